```python
import math
import jax, jax.numpy as jnp
from jax import lax
import numpy as np

D_MODEL = 2048
BATCH = 16
SEQ = 2048
DEPTH = 2

GRID_W = 64
CTX_LEN = 256
Q_BLOCK = 128
ROPE_THETA = 10000.0
EPS = 1e-6
GN_EPS = 1e-5

A_HEADS = D_MODEL // 256
A_DIM = 64
A_VDIM = 2 * A_DIM
B_HEADS = D_MODEL // 512
B_QK = 128
B_V = 2 * B_QK
RET_CHUNK = 128
C_HEADS = D_MODEL // 256
C_KV_HEADS = C_HEADS // 4
C_GROUP = C_HEADS // C_KV_HEADS
C_DIM = 128

N_BRANCH = 3
BRANCH_W = A_HEADS * A_VDIM

IN_SIZES = (A_HEADS * 2 * A_DIM, A_HEADS * 2 * A_DIM, A_HEADS * A_VDIM,
            B_HEADS * B_QK, B_HEADS * B_QK, B_HEADS * B_V, B_HEADS * B_V,
            C_HEADS * C_DIM, C_KV_HEADS * C_DIM, C_KV_HEADS * C_DIM)
W_IN = sum(IN_SIZES)
IN_OFFSETS = tuple(int(o) for o in np.cumsum(IN_SIZES)[:-1])

FF = 256 * ((8 * D_MODEL // 3 + 255) // 256)
N_EXPERTS = 8
TOP_K = 2
N_DENSE = (DEPTH + 1) // 2
N_MOE = DEPTH // 2

kernel_name = 'hybrid_diffattn_retention_gqa_moe_dit'


def rmsnorm(x, g):
    xf = x.astype(jnp.float32)
    y = xf * lax.rsqrt(jnp.mean(xf * xf, axis=-1, keepdims=True) + EPS)
    return (y * g.astype(jnp.float32)).astype(x.dtype)


def modulate(h, shift, scale):
    return h * (1.0 + scale) + shift


def split_heads(t, n):
    B, L, _ = t.shape
    return t.reshape(B, L, n, -1).transpose(0, 2, 1, 3)


def merge_heads(t):
    B, H, L, d = t.shape
    return t.transpose(0, 2, 1, 3).reshape(B, L, H * d)


def axial_rope_tables(length, dim):
    rows = length // GRID_W
    row = jnp.repeat(jnp.arange(rows, dtype=jnp.float32), GRID_W)
    col = jnp.tile(jnp.arange(GRID_W, dtype=jnp.float32), rows)
    n = dim // 4
    inv = ROPE_THETA ** (-jnp.arange(n, dtype=jnp.float32) / n)
    ang = jnp.concatenate([row[:, None] * inv, col[:, None] * inv], axis=-1)
    return jnp.cos(ang), jnp.sin(ang)


def apply_rope(x, cos, sin):
    half = x.shape[-1] // 2
    x1, x2 = x[..., :half], x[..., half:]
    return jnp.concatenate([x1 * cos - x2 * sin, x2 * cos + x1 * sin], axis=-1).astype(x.dtype)


def softmax32(s):
    return jax.nn.softmax(s.astype(jnp.float32), axis=-1)


def to_blocks(t):
    L, d = t.shape[-2], t.shape[-1]
    t = t.reshape(t.shape[:-2] + (L // Q_BLOCK, Q_BLOCK, d))
    return jnp.moveaxis(t, -3, 0)


def from_blocks(t):
    t = jnp.moveaxis(t, 0, -3)
    return t.reshape(t.shape[:-3] + (t.shape[-3] * t.shape[-2], t.shape[-1]))


def sweep_query_blocks(fn, *qs):
    out = lax.map(lambda qb: fn(*qb), tuple(to_blocks(q) for q in qs))
    return from_blocks(out)


def diff_attention(q1, q2, k1, k2, v, lam):
    scale = A_DIM ** -0.5
    def block(q1b, q2b):
        p1 = softmax32(jnp.einsum('bhqd,bhkd->bhqk', q1b, k1) * scale)
        p2 = softmax32(jnp.einsum('bhqd,bhkd->bhqk', q2b, k2) * scale)
        return jnp.einsum('bhqk,bhkd->bhqd', p1 - lam * p2, v)
    return sweep_query_blocks(block, q1, q2)


def gqa_attention(q, k, v):
    scale = C_DIM ** -0.5
    def block(qb):
        p = softmax32(jnp.einsum('bhgqd,bhkd->bhgqk', qb, k) * scale)
        return jnp.einsum('bhgqk,bhkd->bhgqd', p, v)
    return sweep_query_blocks(block, q)


def retention_chunkwise(q, k, v, log_gamma, r0):
    B, H, L, dk = q.shape
    dv = v.shape[-1]
    C = RET_CHUNK
    N = L // C
    qc = q.reshape(B, H, N, C, dk)
    kc = k.reshape(B, H, N, C, dk)
    vc = v.reshape(B, H, N, C, dv)
    lg = log_gamma.astype(jnp.float32)
    idx = jnp.arange(C, dtype=jnp.float32)
    diff = idx[:, None] - idx[None, :]
    decay = jnp.where(diff >= 0, jnp.exp(lg[:, None, None] * jnp.maximum(diff, 0.0)), 0.0)
    scores = jnp.einsum('bhncd,bhnmd->bhncm', qc, kc) * decay[:, None]
    o_intra = jnp.einsum('bhncm,bhnme->bhnce', scores, vc)
    zeta = jnp.exp(lg[:, None] * (C - 1 - idx))
    xi = jnp.exp(lg[:, None] * (idx + 1))
    chunk_state = jnp.einsum('bhncd,hc,bhnce->nbhde', kc, zeta, vc)
    g_chunk = jnp.exp(lg * C)[:, None, None]
    def step(r, u):
        return g_chunk * r + u, r
    _, r_prev = lax.scan(step, r0.astype(jnp.float32), chunk_state)
    o_cross = jnp.einsum('bhncd,hc,nbhde->bhnce', qc, xi, r_prev)
    return (o_intra + o_cross).reshape(B, H, L, dv)


def retention_final_state(k, v, log_gamma):
    L = k.shape[2]
    lg = log_gamma.astype(jnp.float32)
    w = jnp.exp(lg[:, None] * (L - 1 - jnp.arange(L, dtype=jnp.float32)))
    return jnp.einsum('bhld,hl,bhle->bhde', k, w, v)


def retention_bidir(q, k, v, log_decay, r_f, r_b):
    flip = lambda t: jnp.flip(t, axis=2)
    fwd = retention_chunkwise(q, k, v, log_decay[0], r_f)
    bwd = flip(retention_chunkwise(flip(q), flip(k), flip(v), log_decay[1], r_b))
    return fwd + bwd


def head_groupnorm(y):
    yf = y.astype(jnp.float32)
    mu = jnp.mean(yf, axis=-1, keepdims=True)
    var = jnp.mean(jnp.square(yf - mu), axis=-1, keepdims=True)
    return (yf - mu) * lax.rsqrt(var + GN_EPS)


def project_heads(p, qk_g, rope64, rope128):
    B, L, _ = p.shape
    aq, ak, av, bq, bk, bv, bg, cq, ck, cv = jnp.split(p, IN_OFFSETS, axis=-1)
    def rot(t, rope):
        return t if rope is None else apply_rope(t, *rope)
    def diff_pair(t):
        t = rot(t.reshape(B, L, A_HEADS, 2, A_DIM).transpose(3, 0, 2, 1, 4), rope64)
        return t[0], t[1]
    aq1, aq2 = diff_pair(aq)
    ak1, ak2 = diff_pair(ak)
    av = split_heads(av, A_HEADS)
    bq = rot(split_heads(bq, B_HEADS), rope128)
    bk = rot(split_heads(bk, B_HEADS), rope128) * (B_QK ** -0.5)
    bv = split_heads(bv, B_HEADS)
    cq = rot(rmsnorm(split_heads(cq, C_HEADS), qk_g[0]), rope128).reshape(B, C_KV_HEADS, C_GROUP, L, C_DIM)
    ck = rot(rmsnorm(split_heads(ck, C_KV_HEADS), qk_g[1]), rope128)
    cv = split_heads(cv, C_KV_HEADS)
    return (aq1, aq2, ak1, ak2, av, bq, bk, bv, bg, cq, ck, cv)


def run_branches(side, keys, r_f, r_b, lam, lam_init, subln_g, log_decay):
    aq1, aq2, _, _, _, bq, bk, bv, bg, cq, _, _ = side
    ak1, ak2, av, ck, cv = keys
    B, _, L, _ = bq.shape
    o_a = merge_heads(rmsnorm(diff_attention(aq1, aq2, ak1, ak2, av, lam), subln_g) * (1.0 - lam_init))
    o_b = merge_heads(head_groupnorm(retention_bidir(bq, bk, bv, log_decay, r_f, r_b))) * jax.nn.silu(bg)
    o_c = merge_heads(gqa_attention(cq, ck, cv).reshape(B, C_HEADS, L, C_DIM))
    return (o_a, o_b, o_c)


def merge_branches(h, outs, w_branch, w_bgate, w_out):
    gates = jax.nn.sigmoid(h @ w_bgate)
    y = gates[..., :D_MODEL] * (outs[0] @ w_branch[0])
    for i in range(1, N_BRANCH):
        y = y + gates[..., i * D_MODEL:(i + 1) * D_MODEL] * (outs[i] @ w_branch[i])
    return y @ w_out


def token_mixer(h_lat, h_ctx, w_in, qk_g, lam_vec, subln_g, log_decay, w_branch, w_bgate, w_out,
                rope64, rope128, lam_init, with_ctx):
    lat = project_heads(h_lat @ w_in, qk_g, rope64, rope128)
    cx = project_heads(h_ctx @ w_in, qk_g, None, None)
    lv = lam_vec.astype(jnp.float32)
    lam = jnp.exp(jnp.sum(lv[0] * lv[1])) - jnp.exp(jnp.sum(lv[2] * lv[3])) + lam_init
    ak1c, ak2c, avc, bkc, bvc, ckc, cvc = cx[2], cx[3], cx[4], cx[6], cx[7], cx[10], cx[11]
    r_f = retention_final_state(bkc, bvc, log_decay[0])
    r_b = retention_final_state(jnp.flip(bkc, axis=2), jnp.flip(bvc, axis=2), log_decay[1])
    cat = lambda a, b: jnp.concatenate([a, b], axis=2)
    keys_lat = (cat(ak1c, lat[2]), cat(ak2c, lat[3]), cat(avc, lat[4]), cat(ckc, lat[10]), cat(cvc, lat[11]))
    y_lat = merge_branches(h_lat, run_branches(lat, keys_lat, r_f, r_b, lam, lam_init, subln_g, log_decay),
                           w_branch, w_bgate, w_out)
    y_ctx = None
    if with_ctx:
        zeros = jnp.zeros_like(r_f)
        y_ctx = merge_branches(h_ctx, run_branches(cx, (ak1c, ak2c, avc, ckc, cvc), zeros, zeros, lam, lam_init,
                                                   subln_g, log_decay), w_branch, w_bgate, w_out)
    return y_lat, y_ctx


def swiglu(h, w_in, w_out):
    g, u = jnp.split(h @ w_in, 2, axis=-1)
    return (jax.nn.silu(g) * u) @ w_out


def moe_swiglu(h, router, w_in, w_out):
    shp = h.shape
    t = h.reshape(-1, shp[-1])
    logits = (t @ router).astype(jnp.float32)
    top_val, top_idx = lax.top_k(logits, TOP_K)
    w = jax.nn.softmax(top_val, axis=-1)
    gates = jnp.sum(jax.nn.one_hot(top_idx, N_EXPERTS, dtype=jnp.float32) * w[..., None], axis=1)
    y = gates[:, 0:1] * swiglu(t, w_in[0], w_out[0])
    for e in range(1, N_EXPERTS):
        y = y + gates[:, e:e + 1] * swiglu(t, w_in[e], w_out[e])
    return y.reshape(shp)


def channel_mixer(i, h, ffn_w_in, ffn_w_out, moe_router, moe_w_in, moe_w_out):
    j = i // 2
    if i % 2 == 0:
        return swiglu(h, ffn_w_in[j], ffn_w_out[j])
    return moe_swiglu(h, moe_router[j], moe_w_in[j], moe_w_out[j])


def setup_inputs(seed: int = 0) -> dict:
    key = jax.random.key(seed)
    ks = jax.random.split(key, 20)
    nrm = lambda k, shape, s: jax.random.normal(k, shape, jnp.float32) * s
    base_decay = jnp.asarray(np.log1p(-(2.0 ** (-5.0 - np.arange(B_HEADS)))), jnp.float32)
    return {
        'x': nrm(ks[0], (BATCH, SEQ, D_MODEL), 1.0),
        'c': nrm(ks[1], (BATCH, D_MODEL), 1.0),
        'ctx': nrm(ks[2], (BATCH, CTX_LEN, D_MODEL), 1.0),
        'c_ctx': nrm(ks[3], (D_MODEL,), 1.0),
        'w_mod': nrm(ks[4], (DEPTH, D_MODEL, 6 * D_MODEL), 0.5 * D_MODEL ** -0.5),
        'b_mod': nrm(ks[5], (DEPTH, 6 * D_MODEL), 0.02),
        'norm_g': 1.0 + nrm(ks[6], (DEPTH, 4, D_MODEL), 0.05),
        'w_in': nrm(ks[7], (DEPTH, D_MODEL, W_IN), D_MODEL ** -0.5),
        'qk_norm_g': 1.0 + nrm(ks[8], (DEPTH, 2, C_DIM), 0.05),
        'diff_lambda': nrm(ks[9], (DEPTH, 4, A_DIM), 0.1),
        'diff_subln_g': 1.0 + nrm(ks[10], (DEPTH, A_VDIM), 0.05),
        'ret_log_decay': base_decay * jnp.exp(nrm(ks[11], (DEPTH, 2, B_HEADS), 0.1)),
        'w_branch': nrm(ks[12], (DEPTH, N_BRANCH, BRANCH_W, D_MODEL), BRANCH_W ** -0.5),
        'w_branch_gate': nrm(ks[13], (DEPTH, D_MODEL, N_BRANCH * D_MODEL), D_MODEL ** -0.5),
        'w_out': nrm(ks[14], (DEPTH, D_MODEL, D_MODEL), D_MODEL ** -0.5),
        'ffn_w_in': nrm(ks[15], (N_DENSE, D_MODEL, 2 * FF), D_MODEL ** -0.5),
        'ffn_w_out': nrm(ks[16], (N_DENSE, FF, D_MODEL), FF ** -0.5),
        'moe_router': nrm(ks[17], (N_MOE, D_MODEL, N_EXPERTS), D_MODEL ** -0.5),
        'moe_w_in': nrm(ks[18], (N_MOE, N_EXPERTS, D_MODEL, 2 * FF), D_MODEL ** -0.5),
        'moe_w_out': nrm(ks[19], (N_MOE, N_EXPERTS, FF, D_MODEL), FF ** -0.5),
    }


def reference(x, c, ctx, c_ctx, w_mod, b_mod, norm_g, w_in, qk_norm_g, diff_lambda, diff_subln_g,
              ret_log_decay, w_branch, w_branch_gate, w_out, ffn_w_in, ffn_w_out, moe_router,
              moe_w_in, moe_w_out):
    L = x.shape[1]
    rope64 = axial_rope_tables(L, A_DIM)
    rope128 = axial_rope_tables(L, C_DIM)
    cond_lat = jax.nn.silu(c)
    cond_ctx = jax.nn.silu(c_ctx)
    s_ctx = ctx
    for i in range(DEPTH):
        with_ctx = i < DEPTH - 1
        lam_init = 0.8 - 0.6 * math.exp(-0.3 * i)
        m_lat = jnp.split((cond_lat @ w_mod[i] + b_mod[i])[:, None, :], 6, axis=-1)
        m_ctx = jnp.split(cond_ctx @ w_mod[i] + b_mod[i], 6, axis=-1)
        h_lat = modulate(rmsnorm(x, norm_g[i, 0]), m_lat[0], m_lat[1])
        h_ctx = modulate(rmsnorm(s_ctx, norm_g[i, 0]), m_ctx[0], m_ctx[1])
        y_lat, y_ctx = token_mixer(h_lat, h_ctx, w_in[i], qk_norm_g[i], diff_lambda[i], diff_subln_g[i],
                                   ret_log_decay[i], w_branch[i], w_branch_gate[i], w_out[i],
                                   rope64, rope128, lam_init, with_ctx)
        x = x + m_lat[2] * rmsnorm(y_lat, norm_g[i, 1])
        if with_ctx:
            s_ctx = s_ctx + m_ctx[2] * rmsnorm(y_ctx, norm_g[i, 1])
        f_lat = channel_mixer(i, modulate(rmsnorm(x, norm_g[i, 2]), m_lat[3], m_lat[4]),
                              ffn_w_in, ffn_w_out, moe_router, moe_w_in, moe_w_out)
        x = x + m_lat[5] * rmsnorm(f_lat, norm_g[i, 3])
        if with_ctx:
            f_ctx = channel_mixer(i, modulate(rmsnorm(s_ctx, norm_g[i, 2]), m_ctx[3], m_ctx[4]),
                                  ffn_w_in, ffn_w_out, moe_router, moe_w_in, moe_w_out)
            s_ctx = s_ctx + m_ctx[5] * rmsnorm(f_ctx, norm_g[i, 3])
    return x
```

```python
import functools
import math

import jax
import jax.numpy as jnp
from jax import lax
from jax.experimental import pallas as pl
from jax.experimental.pallas import tpu as pltpu

F32 = jnp.float32
BF16 = jnp.bfloat16

GRID_W = 64
ROPE_THETA = 10000.0
EPS = 1e-6
GN_EPS = 1e-5
A_DIM = 64
B_QK = 128
B_V = 256
C_DIM = 128
C_GROUP = 4
TOP_K = 2
LANES = 128
VMEM_LIMIT_BYTES = 56 * 1024 * 1024


def _pick(n, target, mult):
    best = None
    for t in range(mult, min(n, target) + 1, mult):
        if n % t == 0:
            best = t
    return best if best is not None else n


def _params(n_axes):
    return pltpu.CompilerParams(dimension_semantics=("arbitrary",) * n_axes,
                                vmem_limit_bytes=VMEM_LIMIT_BYTES)


def _silu(v):
    return v * jax.nn.sigmoid(v)


def _rms(v, eps):
    return v * lax.rsqrt(jnp.mean(v * v, axis=-1, keepdims=True) + eps)


def _row_is_ctx(tile_idx, tm, n_lat):
    row = tile_idx * tm + lax.broadcasted_iota(jnp.int32, (tm, 1), 0)
    return row >= n_lat


def _modvec_kernel(c_ref, w_ref, b_ref, o_ref):
    a = _silu(c_ref[...]).astype(BF16)
    o_ref[...] = jnp.dot(a, w_ref[...].astype(BF16), preferred_element_type=F32) + b_ref[...]


def _modvec(cond, w_mod, b_mod):
    depth, d, n6 = w_mod.shape
    rows = cond.shape[0]
    tn = _pick(n6, 1024, LANES)
    return pl.pallas_call(
        _modvec_kernel,
        grid=(depth, n6 // tn),
        in_specs=[pl.BlockSpec((rows, d), lambda i, n: (0, 0)),
                  pl.BlockSpec((None, d, tn), lambda i, n: (i, 0, n)),
                  pl.BlockSpec((None, 1, tn), lambda i, n: (i, 0, n))],
        out_specs=pl.BlockSpec((None, rows, tn), lambda i, n: (i, 0, n)),
        out_shape=jax.ShapeDtypeStruct((depth, rows, n6), F32),
        compiler_params=_params(2),
    )(cond, w_mod, b_mod.reshape(depth, 1, n6))


def _norm_mod_kernel(x_ref, g_ref, ml_ref, mc_ref, o_ref, *, k_shift, k_scale, n_lat, tm, has_ctx):
    y = _rms(x_ref[...], EPS) * g_ref[...]
    shift, scale = ml_ref[k_shift], ml_ref[k_scale]
    if has_ctx:
        is_ctx = _row_is_ctx(pl.program_id(1), tm, n_lat)
        shift = jnp.where(is_ctx, mc_ref[k_shift], shift)
        scale = jnp.where(is_ctx, mc_ref[k_scale], scale)
    o_ref[...] = (y * (1.0 + scale) + shift).astype(o_ref.dtype)


def _norm_mod(x, g, mods, rows, n_lat, k_shift, k_scale, out_dtype):
    n_batch, _, d = x.shape
    tm = _pick(rows, 512, 16)
    kern = functools.partial(_norm_mod_kernel, k_shift=k_shift, k_scale=k_scale, n_lat=n_lat, tm=tm,
                             has_ctx=rows > n_lat)
    return pl.pallas_call(
        kern,
        grid=(n_batch, rows // tm),
        in_specs=[pl.BlockSpec((None, tm, d), lambda b, j: (b, j, 0)),
                  pl.BlockSpec((1, d), lambda b, j: (0, 0)),
                  pl.BlockSpec((None, 6, 1, d), lambda b, j: (b, 0, 0, 0)),
                  pl.BlockSpec((None, 6, 1, d), lambda b, j: (n_batch, 0, 0, 0))],
        out_specs=pl.BlockSpec((None, tm, d), lambda b, j: (b, j, 0)),
        out_shape=jax.ShapeDtypeStruct((n_batch, rows, d), out_dtype),
        compiler_params=_params(2),
    )(x, g.reshape(1, d), mods, mods)


def _rope_tables(n_lat, n_ctx):
    rows = n_lat // GRID_W
    row = jnp.repeat(jnp.arange(rows, dtype=F32), GRID_W)
    col = jnp.tile(jnp.arange(GRID_W, dtype=F32), rows)

    def angles(dim):
        n = dim // 4
        inv = ROPE_THETA ** (-jnp.arange(n, dtype=F32) / n)
        return jnp.concatenate([row[:, None] * inv, col[:, None] * inv], axis=-1)

    a64, a128 = angles(A_DIM), angles(C_DIM)
    lane = jnp.arange(LANES)
    c64, s64 = jnp.cos(a64)[:, lane % 32], jnp.sin(a64)[:, lane % 32]
    first_half = (lane % 64) < 32
    sa64 = jnp.where(first_half, -s64, 0.0)
    sb64 = jnp.where(first_half, 0.0, s64)
    c128 = jnp.cos(a128)[:, lane % 64]
    s128 = jnp.sin(a128)[:, lane % 64] * jnp.where(lane < 64, -1.0, 1.0)
    lat = jnp.stack([c64, sa64, sb64, c128, s128])
    ident = jnp.zeros((5, n_ctx, LANES), F32).at[jnp.array([0, 3])].set(1.0)
    return jnp.concatenate([lat, ident], axis=1)


def _proj_epilogue(kind, x, tab_ref, qkg_ref):
    name = kind[0]
    if name == "none":
        return x
    if name == "silu":
        return _silu(x)
    if name == "rope64":
        y = (x * tab_ref[0] + pltpu.roll(x, 96, 1) * tab_ref[1] + pltpu.roll(x, 32, 1) * tab_ref[2])
        return y * kind[1]
    if name == "norm_rope128":
        x = _rms(x, EPS) * qkg_ref[kind[2]:kind[2] + 1, :]
    y = x * tab_ref[3] + pltpu.roll(x, 64, 1) * tab_ref[4]
    return y * kind[1]


def _proj_kernel(h_ref, w_ref, tab_ref, qkg_ref, o_ref, *, groups):
    n = pl.program_id(2)
    acc = jnp.dot(h_ref[...], w_ref[...], preferred_element_type=F32)
    for tiles, kinds in groups:
        cond = functools.reduce(jnp.logical_or, [n == t for t in tiles])

        @pl.when(cond)
        def _(kinds=kinds):
            for c, kind in enumerate(kinds):
                sl = slice(c * LANES, (c + 1) * LANES)
                o_ref[:, sl] = _proj_epilogue(kind, acc[:, sl], tab_ref, qkg_ref).astype(o_ref.dtype)


def _proj_layout(d):
    a_heads, b_heads, c_heads = d // 256, d // 512, d // 256
    c_kv = c_heads // C_GROUP
    segs = [("aq", a_heads, ("rope64", A_DIM ** -0.5)), ("ak", a_heads, ("rope64", 1.0)),
            ("av", a_heads, ("none",)),
            ("bq", b_heads, ("rope128", 1.0)), ("bk", b_heads, ("rope128", B_QK ** -0.5)),
            ("bv", 2 * b_heads, ("none",)), ("bg", 2 * b_heads, ("silu",)),
            ("cq", c_heads, ("norm_rope128", C_DIM ** -0.5, 0)), ("ck", c_kv, ("norm_rope128", 1.0, 1)),
            ("cv", c_kv, ("none",))]
    kinds, off = [], {}
    for name, nblk, kind in segs:
        off[name] = len(kinds)
        kinds += [kind] * nblk
    return kinds, off


def _proj(h, w, tabs, qkg, kinds):
    n_batch, rows, d = h.shape
    width = w.shape[1]
    tm = _pick(rows, 1152, 16)
    tn = _pick(width, 512, LANES)
    per = tn // LANES
    by_kinds = {}
    for t in range(width // tn):
        by_kinds.setdefault(tuple(kinds[t * per:(t + 1) * per]), []).append(t)
    groups = tuple((tuple(tiles), ks) for ks, tiles in by_kinds.items())
    return pl.pallas_call(
        functools.partial(_proj_kernel, groups=groups),
        grid=(n_batch, rows // tm, width // tn),
        in_specs=[pl.BlockSpec((None, tm, d), lambda b, j, n: (b, j, 0)),
                  pl.BlockSpec((d, tn), lambda b, j, n: (0, n)),
                  pl.BlockSpec((5, tm, LANES), lambda b, j, n: (0, j, 0)),
                  pl.BlockSpec((2, LANES), lambda b, j, n: (0, 0))],
        out_specs=pl.BlockSpec((None, tm, tn), lambda b, j, n: (b, j, n)),
        out_shape=jax.ShapeDtypeStruct((n_batch, rows, width), BF16),
        compiler_params=_params(3),
    )(h, w, tabs, qkg)


def _softmax_pv(q, k, v):
    s = lax.dot_general(q, k, (((1,), (1,)), ((), ())), preferred_element_type=F32)
    e = jnp.exp(s - jnp.max(s, axis=-1, keepdims=True))
    den = jnp.sum(e, axis=-1, keepdims=True)
    return jnp.dot(e.astype(BF16), v, preferred_element_type=F32) / den


def _attn_tile(q, k, v, lam_ref, g_ref, diff, lam_init):
    if not diff:
        return _softmax_pv(q, k, v)
    qf = q.astype(F32)
    lane = lax.broadcasted_iota(jnp.int32, qf.shape, 1)
    q1 = jnp.where(lane < A_DIM, qf, 0.0).astype(BF16)
    q2 = jnp.where(lane >= A_DIM, qf, 0.0).astype(BF16)
    lv = lam_ref[...]
    lam = (jnp.exp(jnp.sum(lv[0:1] * lv[1:2], axis=-1, keepdims=True))
           - jnp.exp(jnp.sum(lv[2:3] * lv[3:4], axis=-1, keepdims=True)) + lam_init)
    o = _softmax_pv(q1, k, v) - lam * _softmax_pv(q2, k, v)
    return _rms(o, EPS) * g_ref[...] * (1.0 - lam_init)


def _attn_kernel(q_ref, k_ref, v_ref, lam_ref, g_ref, o_ref, *, diff, lam_init, n_lat_tiles, n_lat, has_ctx):
    def run(k, v):
        o_ref[...] = _attn_tile(q_ref[...], k, v, lam_ref, g_ref, diff, lam_init).astype(o_ref.dtype)

    if not has_ctx:
        run(k_ref[...], v_ref[...])
        return
    i = pl.program_id(2)

    @pl.when(i < n_lat_tiles)
    def _():
        run(k_ref[...], v_ref[...])

    @pl.when(i >= n_lat_tiles)
    def _():
        run(k_ref[n_lat:, :], v_ref[n_lat:, :])


def _attention(p, lam_vec, subln_g, rows, n_lat, n_ctx, n_heads, q_off, k_off, v_off, kv_group, diff, lam_init):
    n_batch, n_keys, _ = p.shape
    has_ctx = rows > n_lat
    tq = _pick(math.gcd(n_lat, n_ctx), 256, 16) if has_ctx else _pick(n_lat, 512, 16)
    kern = functools.partial(_attn_kernel, diff=diff, lam_init=lam_init, n_lat_tiles=n_lat // tq,
                             n_lat=n_lat, has_ctx=has_ctx)
    return pl.pallas_call(
        kern,
        grid=(n_batch, n_heads, rows // tq),
        in_specs=[pl.BlockSpec((None, tq, LANES), lambda b, h, i: (b, i, q_off + h)),
                  pl.BlockSpec((None, n_keys, LANES), lambda b, h, i: (b, 0, k_off + h // kv_group)),
                  pl.BlockSpec((None, n_keys, LANES), lambda b, h, i: (b, 0, v_off + h // kv_group)),
                  pl.BlockSpec(lam_vec.shape, lambda b, h, i: (0, 0)),
                  pl.BlockSpec((1, LANES), lambda b, h, i: (0, 0))],
        out_specs=pl.BlockSpec((None, tq, LANES), lambda b, h, i: (b, i, h)),
        out_shape=jax.ShapeDtypeStruct((n_batch, rows, n_heads * LANES), BF16),
        compiler_params=_params(3),
    )(p, p, p, lam_vec, subln_g.reshape(1, LANES))


def _ret_decay(lgf, lgb, t, pf, pb):
    df = (t - pf).astype(F32)
    db = (pb - t).astype(F32)
    fwd = jnp.where(df >= 0, jnp.exp(lgf * jnp.maximum(df, 0.0)), 0.0)
    bwd = jnp.where(db >= 0, jnp.exp(lgb * jnp.maximum(db, 0.0)), 0.0)
    return fwd + bwd


def _ret_kernel(ld_ref, q_ref, k_ref, v_ref, g_ref, o_ref, d_ref, *, tq, n_keys, n_lat, n_heads, has_ctx):
    h, i, b = pl.program_id(0), pl.program_id(1), pl.program_id(2)
    n_ctx = n_keys - n_lat
    n_lat_tiles = n_lat // tq
    lgf, lgb = ld_ref[h], ld_ref[n_heads + h]
    t_loc = lax.broadcasted_iota(jnp.int32, (tq, 1), 0)

    def lat_decay():
        col = lax.broadcasted_iota(jnp.int32, (1, n_keys), 1)
        pf = jnp.where(col < n_lat, col, col - n_keys)
        d_ref[...] = _ret_decay(lgf, lgb, i * tq + t_loc, pf, col)

    def ctx_decay():
        col = lax.broadcasted_iota(jnp.int32, (1, n_ctx), 1)
        d_ref[:, :n_ctx] = _ret_decay(lgf, lgb, (i - n_lat_tiles) * tq + t_loc, col, col)

    def run(k, v, decay):
        s = lax.dot_general(q_ref[...], k, (((1,), (1,)), ((), ())), preferred_element_type=F32)
        o = jnp.dot((s * decay).astype(BF16), v, preferred_element_type=F32)
        mu = jnp.mean(o, axis=-1, keepdims=True)
        var = jnp.mean(jnp.square(o - mu), axis=-1, keepdims=True)
        o_ref[...] = ((o - mu) * lax.rsqrt(var + GN_EPS) * g_ref[...].astype(F32)).astype(o_ref.dtype)

    if not has_ctx:
        pl.when(b == 0)(lat_decay)
        run(k_ref[...], v_ref[...], d_ref[...])
        return

    pl.when((b == 0) & (i < n_lat_tiles))(lat_decay)
    pl.when((b == 0) & (i >= n_lat_tiles))(ctx_decay)

    @pl.when(i < n_lat_tiles)
    def _():
        run(k_ref[...], v_ref[...], d_ref[...])

    @pl.when(i >= n_lat_tiles)
    def _():
        run(k_ref[n_lat:, :], v_ref[n_lat:, :], d_ref[:, :n_ctx])


def _retention(p, log_decay, rows, n_lat, n_ctx, n_heads, off):
    n_batch, n_keys, _ = p.shape
    has_ctx = rows > n_lat
    tq = _pick(math.gcd(n_lat, n_ctx), 256, 16) if has_ctx else _pick(n_lat, 512, 16)
    kern = functools.partial(_ret_kernel, tq=tq, n_keys=n_keys, n_lat=n_lat, n_heads=n_heads, has_ctx=has_ctx)
    v_off, g_off = off["bv"] // 2, off["bg"] // 2
    grid_spec = pltpu.PrefetchScalarGridSpec(
        num_scalar_prefetch=1,
        grid=(n_heads, rows // tq, n_batch),
        in_specs=[pl.BlockSpec((None, tq, B_QK), lambda h, i, b, ld: (b, i, off["bq"] + h)),
                  pl.BlockSpec((None, n_keys, B_QK), lambda h, i, b, ld: (b, 0, off["bk"] + h)),
                  pl.BlockSpec((None, n_keys, B_V), lambda h, i, b, ld: (b, 0, v_off + h)),
                  pl.BlockSpec((None, tq, B_V), lambda h, i, b, ld: (b, i, g_off + h))],
        out_specs=pl.BlockSpec((None, tq, B_V), lambda h, i, b, ld: (b, i, h)),
        scratch_shapes=[pltpu.VMEM((tq, n_keys), F32)])
    return pl.pallas_call(
        kern, grid_spec=grid_spec,
        out_shape=jax.ShapeDtypeStruct((n_batch, rows, n_heads * B_V), BF16),
        compiler_params=_params(3),
    )(log_decay.reshape(-1).astype(F32), p, p, p, p)


def _merge_kernel(h_ref, oa_ref, ob_ref, oc_ref, ga_ref, gb_ref, gc_ref, wb_ref, y_ref):
    h = h_ref[...]
    y = None
    for idx, (o_ref, wg_ref) in enumerate(((oa_ref, ga_ref), (ob_ref, gb_ref), (oc_ref, gc_ref))):
        gate = jax.nn.sigmoid(jnp.dot(h, wg_ref[...], preferred_element_type=F32))
        term = gate * jnp.dot(o_ref[...], wb_ref[idx], preferred_element_type=F32)
        y = term if y is None else y + term
    y_ref[...] = y.astype(y_ref.dtype)


def _merge(h, outs, w_gate, w_branch, rows):
    n_batch, _, d = h.shape
    bw = w_branch.shape[1]
    tm = _pick(rows, 1152, 16)
    tn = _pick(d, 512, LANES)
    nn = d // tn
    o_spec = pl.BlockSpec((None, tm, bw), lambda b, j, n: (b, j, 0))
    g_specs = [pl.BlockSpec((d, tn), functools.partial(lambda b, j, n, k: (0, k * nn + n), k=k)) for k in range(3)]
    return pl.pallas_call(
        _merge_kernel,
        grid=(n_batch, rows // tm, nn),
        in_specs=[pl.BlockSpec((None, tm, d), lambda b, j, n: (b, j, 0)), o_spec, o_spec, o_spec,
                  *g_specs, pl.BlockSpec((3, bw, tn), lambda b, j, n: (0, 0, n))],
        out_specs=pl.BlockSpec((None, tm, tn), lambda b, j, n: (b, j, n)),
        out_shape=jax.ShapeDtypeStruct((n_batch, rows, d), BF16),
        compiler_params=_params(3),
    )(h, *outs, w_gate, w_gate, w_gate, w_branch)


def _gated_residual(x, f, g_ref, ml_ref, mc_ref, k_gate, tile_idx, tm, n_lat, has_ctx):
    gate = ml_ref[k_gate]
    if has_ctx:
        gate = jnp.where(_row_is_ctx(tile_idx, tm, n_lat), mc_ref[k_gate], gate)
    return x + gate * (_rms(f, EPS) * g_ref[...])


def _out_resid_kernel(y_ref, w_ref, x_ref, g_ref, ml_ref, mc_ref, o_ref, *, k_gate, tm, n_lat, has_ctx):
    f = jnp.dot(y_ref[...], w_ref[...], preferred_element_type=F32)
    o_ref[...] = _gated_residual(x_ref[...], f, g_ref, ml_ref, mc_ref, k_gate, pl.program_id(1), tm, n_lat, has_ctx)


def _out_resid(y, w_out, x, g, mods, rows, n_lat, k_gate):
    n_batch, _, d = y.shape
    tm = _pick(rows, 512, 16)
    kern = functools.partial(_out_resid_kernel, k_gate=k_gate, tm=tm, n_lat=n_lat, has_ctx=rows > n_lat)
    row_spec = pl.BlockSpec((None, tm, d), lambda b, j: (b, j, 0))
    return pl.pallas_call(
        kern,
        grid=(n_batch, rows // tm),
        in_specs=[row_spec, pl.BlockSpec((d, d), lambda b, j: (0, 0)), row_spec,
                  pl.BlockSpec((1, d), lambda b, j: (0, 0)),
                  pl.BlockSpec((None, 6, 1, d), lambda b, j: (b, 0, 0, 0)),
                  pl.BlockSpec((None, 6, 1, d), lambda b, j: (n_batch, 0, 0, 0))],
        out_specs=row_spec,
        out_shape=jax.ShapeDtypeStruct((n_batch, rows, d), F32),
        compiler_params=_params(2),
    )(y, w_out, x, g.reshape(1, d), mods, mods)


def _ffn_kernel(h_ref, wg_ref, wu_ref, wo_ref, x_ref, g_ref, ml_ref, mc_ref, o_ref, acc_ref, *,
                k_gate, tm, n_lat, has_ctx):
    f = pl.program_id(2)
    h = h_ref[...]
    gate = jnp.dot(h, wg_ref[...], preferred_element_type=F32)
    up = jnp.dot(h, wu_ref[...], preferred_element_type=F32)
    part = jnp.dot((_silu(gate) * up).astype(BF16), wo_ref[...], preferred_element_type=F32)

    @pl.when(f == 0)
    def _():
        acc_ref[...] = part

    @pl.when(f > 0)
    def _():
        acc_ref[...] += part

    @pl.when(f == pl.num_programs(2) - 1)
    def _():
        o_ref[...] = _gated_residual(x_ref[...], acc_ref[...], g_ref, ml_ref, mc_ref, k_gate,
                                     pl.program_id(1), tm, n_lat, has_ctx)


def _ffn_dense(h, w_in, w_out, x, g, mods, rows, n_lat, k_gate):
    n_batch, _, d = h.shape
    ff = w_out.shape[0]
    tm = _pick(rows, 576, 16)
    tf = _pick(ff, 512, LANES)
    nf = ff // tf
    kern = functools.partial(_ffn_kernel, k_gate=k_gate, tm=tm, n_lat=n_lat, has_ctx=rows > n_lat)
    row_spec = pl.BlockSpec((None, tm, d), lambda b, j, f: (b, j, 0))
    return pl.pallas_call(
        kern,
        grid=(n_batch, rows // tm, nf),
        in_specs=[row_spec,
                  pl.BlockSpec((d, tf), lambda b, j, f: (0, f)),
                  pl.BlockSpec((d, tf), lambda b, j, f: (0, nf + f)),
                  pl.BlockSpec((tf, d), lambda b, j, f: (f, 0)),
                  row_spec,
                  pl.BlockSpec((1, d), lambda b, j, f: (0, 0)),
                  pl.BlockSpec((None, 6, 1, d), lambda b, j, f: (b, 0, 0, 0)),
                  pl.BlockSpec((None, 6, 1, d), lambda b, j, f: (n_batch, 0, 0, 0))],
        out_specs=row_spec,
        out_shape=jax.ShapeDtypeStruct((n_batch, rows, d), F32),
        scratch_shapes=[pltpu.VMEM((tm, d), F32)],
        compiler_params=_params(3),
    )(h, w_in, w_in, w_out, x, g.reshape(1, d), mods, mods)


def _router_kernel(h_ref, w_ref, idx_ref, wt_ref, *, n_experts):
    logits = jnp.dot(h_ref[...].astype(BF16), w_ref[...], preferred_element_type=F32)
    lane = lax.broadcasted_iota(jnp.int32, logits.shape, 1)
    neg = jnp.float32(-jnp.inf)
    lg = jnp.where(lane < n_experts, logits, neg)
    m1 = jnp.max(lg, axis=-1, keepdims=True)
    i1 = jnp.min(jnp.where(lg == m1, lane, LANES), axis=-1, keepdims=True)
    lg2 = jnp.where(lane == i1, neg, lg)
    m2 = jnp.max(lg2, axis=-1, keepdims=True)
    i2 = jnp.min(jnp.where(lg2 == m2, lane, LANES), axis=-1, keepdims=True)
    e2 = jnp.exp(m2 - m1)
    den = 1.0 + e2
    idx_ref[...] = jnp.where(lane == 0, i1, jnp.where(lane == 1, i2, 0))
    wt_ref[...] = jnp.where(lane == 0, 1.0 / den, jnp.where(lane == 1, e2 / den, 0.0))


def _router(h, w_router):
    n_tok, d = h.shape
    n_experts = w_router.shape[1]
    tm = _pick(n_tok, 1024, 8)
    w_pad = jnp.zeros((d, LANES), BF16).at[:, :n_experts].set(w_router.astype(BF16))
    out_spec = pl.BlockSpec((tm, LANES), lambda i: (i, 0))
    return pl.pallas_call(
        functools.partial(_router_kernel, n_experts=n_experts),
        grid=(n_tok // tm,),
        in_specs=[pl.BlockSpec((tm, d), lambda i: (i, 0)), pl.BlockSpec((d, LANES), lambda i: (0, 0))],
        out_specs=[out_spec, out_spec],
        out_shape=[jax.ShapeDtypeStruct((n_tok, LANES), jnp.int32), jax.ShapeDtypeStruct((n_tok, LANES), F32)],
        compiler_params=_params(1),
    )(h, w_pad)


def _route_plan(top_idx, n_experts, tm, n_tiles):
    n_tok = top_idx.shape[0]
    e_flat = top_idx.reshape(-1)
    onehot = (e_flat[:, None] == jnp.arange(n_experts, dtype=jnp.int32)[None, :]).astype(jnp.int32)
    csum = jnp.cumsum(onehot, axis=0)
    rank = jnp.sum((csum - onehot) * onehot, axis=1)
    tiles_e = (csum[-1] + tm - 1) // tm
    tiles_end = jnp.cumsum(tiles_e)
    pos = (tiles_end - tiles_e)[e_flat] * tm + rank
    n_used = tiles_end[-1]
    tile_ids = jnp.arange(n_tiles, dtype=jnp.int32)
    tile_expert = jnp.minimum(jnp.searchsorted(tiles_end, tile_ids, side="right"), n_experts - 1).astype(jnp.int32)
    last_expert = tile_expert[jnp.maximum(n_used - 1, 0)]
    tile_expert = jnp.where(tile_ids < n_used, tile_expert, last_expert)
    row_token = jnp.zeros((n_tiles * tm,), jnp.int32).at[pos].set(jnp.arange(TOP_K * n_tok, dtype=jnp.int32) // TOP_K)
    return row_token, pos.reshape(n_tok, TOP_K).astype(jnp.int32), tile_expert, n_used.reshape(1).astype(jnp.int32)


def _gather_kernel(tok_ref, src_ref, dst_ref, sem, *, tg):
    base = pl.program_id(0) * tg

    def issue(r, carry):
        pltpu.make_async_copy(src_ref.at[pl.ds(tok_ref[0, r], 1)], dst_ref.at[pl.ds(base + r, 1)], sem).start()
        return carry

    def drain(r, carry):
        pltpu.make_async_copy(src_ref.at[pl.ds(0, 1)], dst_ref.at[pl.ds(base + r, 1)], sem).wait()
        return carry

    lax.fori_loop(0, tg, issue, 0)
    lax.fori_loop(0, tg, drain, 0)


def _gather_rows(src, row_token, tg):
    n_rows = row_token.shape[0]
    d = src.shape[1]
    return pl.pallas_call(
        functools.partial(_gather_kernel, tg=tg),
        grid=(n_rows // tg,),
        in_specs=[pl.BlockSpec((None, 1, tg), lambda i: (i, 0, 0), memory_space=pltpu.SMEM),
                  pl.BlockSpec(memory_space=pl.ANY)],
        out_specs=pl.BlockSpec(memory_space=pl.ANY),
        out_shape=jax.ShapeDtypeStruct((n_rows, d), src.dtype),
        scratch_shapes=[pltpu.SemaphoreType.DMA(())],
        compiler_params=_params(1),
    )(row_token.reshape(n_rows // tg, 1, tg), src)


def _moe_ffn_kernel(te_ref, nv_ref, xs_ref, wg_ref, wu_ref, wo_ref, o_ref, xb_ref):
    i, f = pl.program_id(0), pl.program_id(1)
    valid = i < nv_ref[0]

    @pl.when(f == 0)
    def _():
        xb_ref[...] = xs_ref[...].astype(BF16)
        o_ref[...] = jnp.zeros_like(o_ref)

    @pl.when(valid)
    def _():
        xb = xb_ref[...]
        gate = jnp.dot(xb, wg_ref[...], preferred_element_type=F32)
        up = jnp.dot(xb, wu_ref[...], preferred_element_type=F32)
        o_ref[...] += jnp.dot((_silu(gate) * up).astype(BF16), wo_ref[...], preferred_element_type=F32)


def _moe_ffn(xs, w_in, w_out, tile_expert, n_used, tm):
    n_rows, d = xs.shape
    ff = w_out.shape[1]
    tf = _pick(ff, 512, LANES)
    nf = ff // tf

    def f_eff(i, f, nv):
        return jnp.where(i < nv[0], f, nf - 1)

    grid_spec = pltpu.PrefetchScalarGridSpec(
        num_scalar_prefetch=2,
        grid=(n_rows // tm, nf),
        in_specs=[pl.BlockSpec((tm, d), lambda i, f, te, nv: (jnp.minimum(i, jnp.maximum(nv[0] - 1, 0)), 0)),
                  pl.BlockSpec((None, d, tf), lambda i, f, te, nv: (te[i], 0, f_eff(i, f, nv))),
                  pl.BlockSpec((None, d, tf), lambda i, f, te, nv: (te[i], 0, nf + f_eff(i, f, nv))),
                  pl.BlockSpec((None, tf, d), lambda i, f, te, nv: (te[i], f_eff(i, f, nv), 0))],
        out_specs=pl.BlockSpec((tm, d), lambda i, f, te, nv: (i, 0)),
        scratch_shapes=[pltpu.VMEM((tm, d), BF16)])
    return pl.pallas_call(
        _moe_ffn_kernel, grid_spec=grid_spec,
        out_shape=jax.ShapeDtypeStruct((n_rows, d), F32),
        compiler_params=_params(2),
    )(tile_expert, n_used, xs, w_in, w_in, w_out)


def _combine_kernel(pos_ref, ys_ref, wt_ref, x_ref, g_ref, ml_ref, mc_ref, o_ref, buf_ref, sem, *,
                    k_gate, tc, n_lat, has_ctx):
    def issue(r, carry):
        for k in range(TOP_K):
            pltpu.make_async_copy(ys_ref.at[pl.ds(pos_ref[0, TOP_K * r + k], 1)],
                                  buf_ref.at[k, pl.ds(r, 1)], sem).start()
        return carry

    def drain(r, carry):
        for k in range(TOP_K):
            pltpu.make_async_copy(ys_ref.at[pl.ds(0, 1)], buf_ref.at[k, pl.ds(r, 1)], sem).wait()
        return carry

    lax.fori_loop(0, tc, issue, 0)
    lax.fori_loop(0, tc, drain, 0)
    wt = wt_ref[...]
    f = wt[:, 0:1] * buf_ref[0]
    for k in range(1, TOP_K):
        f = f + wt[:, k:k + 1] * buf_ref[k]
    o_ref[...] = _gated_residual(x_ref[...], f, g_ref, ml_ref, mc_ref, k_gate, pl.program_id(1), tc, n_lat, has_ctx)


def _combine(ys, pos, wts, x, g, mods, rows, n_lat, k_gate):
    n_batch, _, d = x.shape
    tc = _pick(rows, 256, 8)
    nj = rows // tc
    kern = functools.partial(_combine_kernel, k_gate=k_gate, tc=tc, n_lat=n_lat, has_ctx=rows > n_lat)
    row_spec = pl.BlockSpec((None, tc, d), lambda b, j: (b, j, 0))
    return pl.pallas_call(
        kern,
        grid=(n_batch, nj),
        in_specs=[pl.BlockSpec((None, 1, TOP_K * tc), lambda b, j: (b * nj + j, 0, 0), memory_space=pltpu.SMEM),
                  pl.BlockSpec(memory_space=pl.ANY),
                  pl.BlockSpec((tc, LANES), lambda b, j: (b * nj + j, 0)),
                  row_spec,
                  pl.BlockSpec((1, d), lambda b, j: (0, 0)),
                  pl.BlockSpec((None, 6, 1, d), lambda b, j: (b, 0, 0, 0)),
                  pl.BlockSpec((None, 6, 1, d), lambda b, j: (n_batch, 0, 0, 0))],
        out_specs=row_spec,
        out_shape=jax.ShapeDtypeStruct((n_batch, rows, d), F32),
        scratch_shapes=[pltpu.VMEM((TOP_K, tc, d), F32), pltpu.SemaphoreType.DMA(())],
        compiler_params=_params(2),
    )(pos.reshape(n_batch * nj, 1, TOP_K * tc), ys, wts, x, g.reshape(1, d), mods, mods)


def _ffn_moe(h, w_router, w_in, w_out, x, g, mods, rows, n_lat, k_gate):
    n_batch, _, d = h.shape
    n_tok = n_batch * rows
    n_experts = w_router.shape[1]
    tm = min(768, max(16, (TOP_K * n_tok // n_experts) // 16 * 16))
    n_tiles = (TOP_K * n_tok + n_experts * (tm - 1)) // tm
    h_flat = h.reshape(n_tok, d)
    idx, wts = _router(h_flat, w_router)
    row_token, pos, tile_expert, n_used = _route_plan(idx[:, :TOP_K], n_experts, tm, n_tiles)
    xs = _gather_rows(h_flat, row_token, tm)
    ys = _moe_ffn(xs, w_in, w_out, tile_expert, n_used, tm)
    return _combine(ys, pos, wts, x, g, mods, rows, n_lat, k_gate)


def kernel(x, c, ctx, c_ctx, w_mod, b_mod, norm_g, w_in, qk_norm_g, diff_lambda, diff_subln_g, ret_log_decay,
           w_branch, w_branch_gate, w_out, ffn_w_in, ffn_w_out, moe_router, moe_w_in, moe_w_out):
    n_batch, n_lat, d = x.shape
    n_ctx = ctx.shape[1]
    depth = w_mod.shape[0]
    a_heads, b_heads, c_heads = d // 256, d // 512, d // 256
    kinds, off = _proj_layout(d)

    mod_rows = (n_batch + 1 + 7) // 8 * 8
    cond = jnp.zeros((mod_rows, d), F32).at[:n_batch].set(c).at[n_batch].set(c_ctx)
    mods_all = _modvec(cond, w_mod, b_mod).reshape(depth, mod_rows, 6, 1, d)
    tabs = _rope_tables(n_lat, n_ctx)

    xs = jnp.concatenate([x, ctx], axis=1)
    n_all = n_lat + n_ctx
    for i in range(depth):
        rows = n_all if i < depth - 1 else n_lat
        lam_init = 0.8 - 0.6 * math.exp(-0.3 * i)
        mods = mods_all[i]
        h = _norm_mod(xs, norm_g[i, 0], mods, n_all, n_lat, 0, 1, BF16)
        p = _proj(h, w_in[i].astype(BF16), tabs, qk_norm_g[i], kinds)
        o_a = _attention(p, diff_lambda[i], diff_subln_g[i], rows, n_lat, n_ctx, a_heads,
                         off["aq"], off["ak"], off["av"], 1, True, lam_init)
        o_b = _retention(p, ret_log_decay[i], rows, n_lat, n_ctx, b_heads, off)
        o_c = _attention(p, diff_lambda[i], diff_subln_g[i], rows, n_lat, n_ctx, c_heads,
                         off["cq"], off["ck"], off["cv"], C_GROUP, False, lam_init)
        y = _merge(h, (o_a, o_b, o_c), w_branch_gate[i].astype(BF16), w_branch[i].astype(BF16), rows)
        xs = _out_resid(y, w_out[i].astype(BF16), xs, norm_g[i, 1], mods, rows, n_lat, 2)
        j = i // 2
        if i % 2 == 0:
            h2 = _norm_mod(xs, norm_g[i, 2], mods, rows, n_lat, 3, 4, BF16)
            xs = _ffn_dense(h2, ffn_w_in[j].astype(BF16), ffn_w_out[j].astype(BF16), xs, norm_g[i, 3], mods,
                            rows, n_lat, 5)
        else:
            h2 = _norm_mod(xs, norm_g[i, 2], mods, rows, n_lat, 3, 4, F32)
            xs = _ffn_moe(h2, moe_router[j], moe_w_in[j].astype(BF16), moe_w_out[j].astype(BF16), xs,
                          norm_g[i, 3], mods, rows, n_lat, 5)
    return xs[:, :n_lat]
```

```python
import functools
import math

import jax
import jax.numpy as jnp
from jax import lax
from jax.experimental import pallas as pl
from jax.experimental.pallas import tpu as pltpu

F32 = jnp.float32
BF16 = jnp.bfloat16

GRID_W = 64
ROPE_THETA = 10000.0
EPS = 1e-6
GN_EPS = 1e-5
A_DIM = 64
B_QK = 128
B_V = 256
C_DIM = 128
C_GROUP = 4
TOP_K = 2
LANES = 128
VMEM_LIMIT_BYTES = 56 * 1024 * 1024


def _pick(n, target, mult):
    best = None
    for t in range(mult, min(n, target) + 1, mult):
        if n % t == 0:
            best = t
    return best if best is not None else n


def _params(n_axes):
    return pltpu.CompilerParams(dimension_semantics=("arbitrary",) * n_axes,
                                vmem_limit_bytes=VMEM_LIMIT_BYTES)


def _silu(v):
    return v * jax.nn.sigmoid(v)


def _rms(v, eps):
    return v * lax.rsqrt(jnp.mean(v * v, axis=-1, keepdims=True) + eps)


def _row_is_ctx(tile_idx, tm, n_lat):
    row = tile_idx * tm + lax.broadcasted_iota(jnp.int32, (tm, 1), 0)
    return row >= n_lat


def _modvec_kernel(c_ref, w_ref, b_ref, o_ref):
    a = _silu(c_ref[...]).astype(BF16)
    o_ref[...] = jnp.dot(a, w_ref[...].astype(BF16), preferred_element_type=F32) + b_ref[...]


def _modvec(cond, w_mod, b_mod):
    depth, d, n6 = w_mod.shape
    rows = cond.shape[0]
    tn = _pick(n6, 1024, LANES)
    return pl.pallas_call(
        _modvec_kernel,
        grid=(depth, n6 // tn),
        in_specs=[pl.BlockSpec((rows, d), lambda i, n: (0, 0)),
                  pl.BlockSpec((None, d, tn), lambda i, n: (i, 0, n)),
                  pl.BlockSpec((None, 1, tn), lambda i, n: (i, 0, n))],
        out_specs=pl.BlockSpec((None, rows, tn), lambda i, n: (i, 0, n)),
        out_shape=jax.ShapeDtypeStruct((depth, rows, n6), F32),
        compiler_params=_params(2),
    )(cond, w_mod, b_mod.reshape(depth, 1, n6))


def _norm_mod_kernel(x_ref, g_ref, ml_ref, mc_ref, o_ref, *, k_shift, k_scale, n_lat, tm, has_ctx):
    y = _rms(x_ref[...], EPS) * g_ref[...]
    shift, scale = ml_ref[k_shift], ml_ref[k_scale]
    if has_ctx:
        is_ctx = _row_is_ctx(pl.program_id(1), tm, n_lat)
        shift = jnp.where(is_ctx, mc_ref[k_shift], shift)
        scale = jnp.where(is_ctx, mc_ref[k_scale], scale)
    o_ref[...] = (y * (1.0 + scale) + shift).astype(o_ref.dtype)


def _norm_mod(x, g, mods, rows, n_lat, k_shift, k_scale, out_dtype):
    n_batch, _, d = x.shape
    tm = _pick(rows, 512, 16)
    kern = functools.partial(_norm_mod_kernel, k_shift=k_shift, k_scale=k_scale, n_lat=n_lat, tm=tm,
                             has_ctx=rows > n_lat)
    return pl.pallas_call(
        kern,
        grid=(n_batch, rows // tm),
        in_specs=[pl.BlockSpec((None, tm, d), lambda b, j: (b, j, 0)),
                  pl.BlockSpec((1, d), lambda b, j: (0, 0)),
                  pl.BlockSpec((None, 6, 1, d), lambda b, j: (b, 0, 0, 0)),
                  pl.BlockSpec((None, 6, 1, d), lambda b, j: (n_batch, 0, 0, 0))],
        out_specs=pl.BlockSpec((None, tm, d), lambda b, j: (b, j, 0)),
        out_shape=jax.ShapeDtypeStruct((n_batch, rows, d), out_dtype),
        compiler_params=_params(2),
    )(x, g.reshape(1, d), mods, mods)


def _rope_tables(n_lat, n_ctx):
    rows = n_lat // GRID_W
    row = jnp.repeat(jnp.arange(rows, dtype=F32), GRID_W)
    col = jnp.tile(jnp.arange(GRID_W, dtype=F32), rows)

    def angles(dim):
        n = dim // 4
        inv = ROPE_THETA ** (-jnp.arange(n, dtype=F32) / n)
        return jnp.concatenate([row[:, None] * inv, col[:, None] * inv], axis=-1)

    a64, a128 = angles(A_DIM), angles(C_DIM)
    lane = jnp.arange(LANES)
    c64, s64 = jnp.cos(a64)[:, lane % 32], jnp.sin(a64)[:, lane % 32]
    first_half = (lane % 64) < 32
    sa64 = jnp.where(first_half, -s64, 0.0)
    sb64 = jnp.where(first_half, 0.0, s64)
    c128 = jnp.cos(a128)[:, lane % 64]
    s128 = jnp.sin(a128)[:, lane % 64] * jnp.where(lane < 64, -1.0, 1.0)
    lat = jnp.stack([c64, sa64, sb64, c128, s128])
    ident = jnp.zeros((5, n_ctx, LANES), F32).at[jnp.array([0, 3])].set(1.0)
    return jnp.concatenate([lat, ident], axis=1)


def _proj_epilogue(kind, x, tab_ref, qkg_ref):
    name = kind[0]
    if name == "none":
        return x
    if name == "silu":
        return _silu(x)
    if name == "rope64":
        y = (x * tab_ref[0] + pltpu.roll(x, 96, 1) * tab_ref[1] + pltpu.roll(x, 32, 1) * tab_ref[2])
        return y * kind[1]
    if name == "norm_rope128":
        x = _rms(x, EPS) * qkg_ref[kind[2]:kind[2] + 1, :]
    y = x * tab_ref[3] + pltpu.roll(x, 64, 1) * tab_ref[4]
    return y * kind[1]


def _proj_kernel(h_ref, w_ref, tab_ref, qkg_ref, o_ref, *, groups):
    n = pl.program_id(2)
    acc = jnp.dot(h_ref[...], w_ref[...], preferred_element_type=F32)
    for tiles, kinds in groups:
        cond = functools.reduce(jnp.logical_or, [n == t for t in tiles])

        @pl.when(cond)
        def _(kinds=kinds):
            for c, kind in enumerate(kinds):
                sl = slice(c * LANES, (c + 1) * LANES)
                o_ref[:, sl] = _proj_epilogue(kind, acc[:, sl], tab_ref, qkg_ref).astype(o_ref.dtype)


def _proj_layout(d):
    a_heads, b_heads, c_heads = d // 256, d // 512, d // 256
    c_kv = c_heads // C_GROUP
    segs = [("aq", a_heads, ("rope64", A_DIM ** -0.5)), ("ak", a_heads, ("rope64", 1.0)),
            ("av", a_heads, ("none",)),
            ("bq", b_heads, ("rope128", 1.0)), ("bk", b_heads, ("rope128", B_QK ** -0.5)),
            ("bv", 2 * b_heads, ("none",)), ("bg", 2 * b_heads, ("silu",)),
            ("cq", c_heads, ("norm_rope128", C_DIM ** -0.5, 0)), ("ck", c_kv, ("norm_rope128", 1.0, 1)),
            ("cv", c_kv, ("none",))]
    kinds, off = [], {}
    for name, nblk, kind in segs:
        off[name] = len(kinds)
        kinds += [kind] * nblk
    return kinds, off


def _proj(h, w, tabs, qkg, kinds):
    n_batch, rows, d = h.shape
    width = w.shape[1]
    tm = _pick(rows, 1152, 16)
    tn = _pick(width, 512, LANES)
    per = tn // LANES
    by_kinds = {}
    for t in range(width // tn):
        by_kinds.setdefault(tuple(kinds[t * per:(t + 1) * per]), []).append(t)
    groups = tuple((tuple(tiles), ks) for ks, tiles in by_kinds.items())
    return pl.pallas_call(
        functools.partial(_proj_kernel, groups=groups),
        grid=(n_batch, rows // tm, width // tn),
        in_specs=[pl.BlockSpec((None, tm, d), lambda b, j, n: (b, j, 0)),
                  pl.BlockSpec((d, tn), lambda b, j, n: (0, n)),
                  pl.BlockSpec((5, tm, LANES), lambda b, j, n: (0, j, 0)),
                  pl.BlockSpec((2, LANES), lambda b, j, n: (0, 0))],
        out_specs=pl.BlockSpec((None, tm, tn), lambda b, j, n: (b, j, n)),
        out_shape=jax.ShapeDtypeStruct((n_batch, rows, width), BF16),
        compiler_params=_params(3),
    )(h, w, tabs, qkg)


def _softmax_pv(q, k, v):
    s = lax.dot_general(q, k, (((1,), (1,)), ((), ())), preferred_element_type=F32)
    e = jnp.exp(s - jnp.max(s, axis=-1, keepdims=True))
    den = jnp.sum(e, axis=-1, keepdims=True)
    return jnp.dot(e.astype(BF16), v, preferred_element_type=F32) / den


def _attn_tile(q, k, v, lam_ref, g_ref, diff, lam_init):
    if not diff:
        return _softmax_pv(q, k, v)
    qf = q.astype(F32)
    lane = lax.broadcasted_iota(jnp.int32, qf.shape, 1)
    q1 = jnp.where(lane < A_DIM, qf, 0.0).astype(BF16)
    q2 = jnp.where(lane >= A_DIM, qf, 0.0).astype(BF16)
    lv = lam_ref[...]
    lam = (jnp.exp(jnp.sum(lv[0:1] * lv[1:2], axis=-1, keepdims=True))
           - jnp.exp(jnp.sum(lv[2:3] * lv[3:4], axis=-1, keepdims=True)) + lam_init)
    o = _softmax_pv(q1, k, v) - lam * _softmax_pv(q2, k, v)
    return _rms(o, EPS) * g_ref[...] * (1.0 - lam_init)


def _attn_kernel(q_ref, k_ref, v_ref, lam_ref, g_ref, o_ref, *, diff, lam_init, n_lat_tiles, n_lat, has_ctx):
    def run(k, v):
        o_ref[...] = _attn_tile(q_ref[...], k, v, lam_ref, g_ref, diff, lam_init).astype(o_ref.dtype)

    if not has_ctx:
        run(k_ref[...], v_ref[...])
        return
    i = pl.program_id(2)

    @pl.when(i < n_lat_tiles)
    def _():
        run(k_ref[...], v_ref[...])

    @pl.when(i >= n_lat_tiles)
    def _():
        run(k_ref[n_lat:, :], v_ref[n_lat:, :])


def _attention(p, lam_vec, subln_g, rows, n_lat, n_ctx, n_heads, q_off, k_off, v_off, kv_group, diff, lam_init):
    n_batch, n_keys, _ = p.shape
    has_ctx = rows > n_lat
    tq = _pick(math.gcd(n_lat, n_ctx), 256, 16) if has_ctx else _pick(n_lat, 512, 16)
    kern = functools.partial(_attn_kernel, diff=diff, lam_init=lam_init, n_lat_tiles=n_lat // tq,
                             n_lat=n_lat, has_ctx=has_ctx)
    return pl.pallas_call(
        kern,
        grid=(n_batch, n_heads, rows // tq),
        in_specs=[pl.BlockSpec((None, tq, LANES), lambda b, h, i: (b, i, q_off + h)),
                  pl.BlockSpec((None, n_keys, LANES), lambda b, h, i: (b, 0, k_off + h // kv_group)),
                  pl.BlockSpec((None, n_keys, LANES), lambda b, h, i: (b, 0, v_off + h // kv_group)),
                  pl.BlockSpec(lam_vec.shape, lambda b, h, i: (0, 0)),
                  pl.BlockSpec((1, LANES), lambda b, h, i: (0, 0))],
        out_specs=pl.BlockSpec((None, tq, LANES), lambda b, h, i: (b, i, h)),
        out_shape=jax.ShapeDtypeStruct((n_batch, rows, n_heads * LANES), BF16),
        compiler_params=_params(3),
    )(p, p, p, lam_vec, subln_g.reshape(1, LANES))


def _ret_decay(lgf, lgb, t, pf, pb):
    df = (t - pf).astype(F32)
    db = (pb - t).astype(F32)
    fwd = jnp.where(df >= 0, jnp.exp(lgf * jnp.maximum(df, 0.0)), 0.0)
    bwd = jnp.where(db >= 0, jnp.exp(lgb * jnp.maximum(db, 0.0)), 0.0)
    return fwd + bwd


def _ret_kernel(ld_ref, q_ref, k_ref, v_ref, g_ref, o_ref, d_ref, *, tq, n_keys, n_lat, n_heads, has_ctx):
    h, i, b = pl.program_id(0), pl.program_id(1), pl.program_id(2)
    n_ctx = n_keys - n_lat
    n_lat_tiles = n_lat // tq
    lgf, lgb = ld_ref[h], ld_ref[n_heads + h]
    t_loc = lax.broadcasted_iota(jnp.int32, (tq, 1), 0)

    def lat_decay():
        col = lax.broadcasted_iota(jnp.int32, (1, n_keys), 1)
        pf = jnp.where(col < n_lat, col, col - n_keys)
        d_ref[...] = _ret_decay(lgf, lgb, i * tq + t_loc, pf, col)

    def ctx_decay():
        col = lax.broadcasted_iota(jnp.int32, (1, n_ctx), 1)
        d_ref[:, :n_ctx] = _ret_decay(lgf, lgb, (i - n_lat_tiles) * tq + t_loc, col, col)

    def run(k, v, decay):
        s = lax.dot_general(q_ref[...], k, (((1,), (1,)), ((), ())), preferred_element_type=F32)
        o = jnp.dot((s * decay).astype(BF16), v, preferred_element_type=F32)
        mu = jnp.mean(o, axis=-1, keepdims=True)
        var = jnp.mean(jnp.square(o - mu), axis=-1, keepdims=True)
        o_ref[...] = ((o - mu) * lax.rsqrt(var + GN_EPS) * g_ref[...].astype(F32)).astype(o_ref.dtype)

    if not has_ctx:
        pl.when(b == 0)(lat_decay)
        run(k_ref[...], v_ref[...], d_ref[...])
        return

    pl.when((b == 0) & (i < n_lat_tiles))(lat_decay)
    pl.when((b == 0) & (i >= n_lat_tiles))(ctx_decay)

    @pl.when(i < n_lat_tiles)
    def _():
        run(k_ref[...], v_ref[...], d_ref[...])

    @pl.when(i >= n_lat_tiles)
    def _():
        run(k_ref[n_lat:, :], v_ref[n_lat:, :], d_ref[:, :n_ctx])


def _retention(p, log_decay, rows, n_lat, n_ctx, n_heads, off):
    n_batch, n_keys, _ = p.shape
    has_ctx = rows > n_lat
    tq = _pick(math.gcd(n_lat, n_ctx), 256, 16) if has_ctx else _pick(n_lat, 512, 16)
    kern = functools.partial(_ret_kernel, tq=tq, n_keys=n_keys, n_lat=n_lat, n_heads=n_heads, has_ctx=has_ctx)
    v_off, g_off = off["bv"] // 2, off["bg"] // 2
    grid_spec = pltpu.PrefetchScalarGridSpec(
        num_scalar_prefetch=1,
        grid=(n_heads, rows // tq, n_batch),
        in_specs=[pl.BlockSpec((None, tq, B_QK), lambda h, i, b, ld: (b, i, off["bq"] + h)),
                  pl.BlockSpec((None, n_keys, B_QK), lambda h, i, b, ld: (b, 0, off["bk"] + h)),
                  pl.BlockSpec((None, n_keys, B_V), lambda h, i, b, ld: (b, 0, v_off + h)),
                  pl.BlockSpec((None, tq, B_V), lambda h, i, b, ld: (b, i, g_off + h))],
        out_specs=pl.BlockSpec((None, tq, B_V), lambda h, i, b, ld: (b, i, h)),
        scratch_shapes=[pltpu.VMEM((tq, n_keys), F32)])
    return pl.pallas_call(
        kern, grid_spec=grid_spec,
        out_shape=jax.ShapeDtypeStruct((n_batch, rows, n_heads * B_V), BF16),
        compiler_params=_params(3),
    )(log_decay.reshape(-1).astype(F32), p, p, p, p)


def _merge_kernel(h_ref, oa_ref, ob_ref, oc_ref, ga_ref, gb_ref, gc_ref, wb_ref, y_ref):
    h = h_ref[...]
    y = None
    for idx, (o_ref, wg_ref) in enumerate(((oa_ref, ga_ref), (ob_ref, gb_ref), (oc_ref, gc_ref))):
        gate = jax.nn.sigmoid(jnp.dot(h, wg_ref[...], preferred_element_type=F32))
        term = gate * jnp.dot(o_ref[...], wb_ref[idx], preferred_element_type=F32)
        y = term if y is None else y + term
    y_ref[...] = y.astype(y_ref.dtype)


def _merge(h, outs, w_gate, w_branch, rows):
    n_batch, _, d = h.shape
    bw = w_branch.shape[1]
    tm = _pick(rows, 1152, 16)
    tn = _pick(d, 512, LANES)
    nn = d // tn
    o_spec = pl.BlockSpec((None, tm, bw), lambda b, j, n: (b, j, 0))
    g_specs = [pl.BlockSpec((d, tn), functools.partial(lambda b, j, n, k: (0, k * nn + n), k=k)) for k in range(3)]
    return pl.pallas_call(
        _merge_kernel,
        grid=(n_batch, rows // tm, nn),
        in_specs=[pl.BlockSpec((None, tm, d), lambda b, j, n: (b, j, 0)), o_spec, o_spec, o_spec,
                  *g_specs, pl.BlockSpec((3, bw, tn), lambda b, j, n: (0, 0, n))],
        out_specs=pl.BlockSpec((None, tm, tn), lambda b, j, n: (b, j, n)),
        out_shape=jax.ShapeDtypeStruct((n_batch, rows, d), BF16),
        compiler_params=_params(3),
    )(h, *outs, w_gate, w_gate, w_gate, w_branch)


def _gated_residual(x, f, g_ref, ml_ref, mc_ref, k_gate, tile_idx, tm, n_lat, has_ctx):
    gate = ml_ref[k_gate]
    if has_ctx:
        gate = jnp.where(_row_is_ctx(tile_idx, tm, n_lat), mc_ref[k_gate], gate)
    return x + gate * (_rms(f, EPS) * g_ref[...])


def _out_resid_kernel(y_ref, w_ref, x_ref, g_ref, ml_ref, mc_ref, o_ref, *, k_gate, tm, n_lat, has_ctx):
    f = jnp.dot(y_ref[...], w_ref[...], preferred_element_type=F32)
    o_ref[...] = _gated_residual(x_ref[...], f, g_ref, ml_ref, mc_ref, k_gate, pl.program_id(1), tm, n_lat, has_ctx)


def _out_resid(y, w_out, x, g, mods, rows, n_lat, k_gate):
    n_batch, _, d = y.shape
    tm = _pick(rows, 512, 16)
    kern = functools.partial(_out_resid_kernel, k_gate=k_gate, tm=tm, n_lat=n_lat, has_ctx=rows > n_lat)
    row_spec = pl.BlockSpec((None, tm, d), lambda b, j: (b, j, 0))
    return pl.pallas_call(
        kern,
        grid=(n_batch, rows // tm),
        in_specs=[row_spec, pl.BlockSpec((d, d), lambda b, j: (0, 0)), row_spec,
                  pl.BlockSpec((1, d), lambda b, j: (0, 0)),
                  pl.BlockSpec((None, 6, 1, d), lambda b, j: (b, 0, 0, 0)),
                  pl.BlockSpec((None, 6, 1, d), lambda b, j: (n_batch, 0, 0, 0))],
        out_specs=row_spec,
        out_shape=jax.ShapeDtypeStruct((n_batch, rows, d), F32),
        compiler_params=_params(2),
    )(y, w_out, x, g.reshape(1, d), mods, mods)


def _ffn_kernel(h_ref, wg_ref, wu_ref, wo_ref, x_ref, g_ref, ml_ref, mc_ref, o_ref, acc_ref, *,
                k_gate, tm, n_lat, has_ctx):
    f = pl.program_id(2)
    h = h_ref[...]
    gate = jnp.dot(h, wg_ref[...], preferred_element_type=F32)
    up = jnp.dot(h, wu_ref[...], preferred_element_type=F32)
    part = jnp.dot((_silu(gate) * up).astype(BF16), wo_ref[...], preferred_element_type=F32)

    @pl.when(f == 0)
    def _():
        acc_ref[...] = part

    @pl.when(f > 0)
    def _():
        acc_ref[...] += part

    @pl.when(f == pl.num_programs(2) - 1)
    def _():
        o_ref[...] = _gated_residual(x_ref[...], acc_ref[...], g_ref, ml_ref, mc_ref, k_gate,
                                     pl.program_id(1), tm, n_lat, has_ctx)


def _ffn_dense(h, w_in, w_out, x, g, mods, rows, n_lat, k_gate):
    n_batch, _, d = h.shape
    ff = w_out.shape[0]
    tm = _pick(rows, 576, 16)
    tf = _pick(ff, 512, LANES)
    nf = ff // tf
    kern = functools.partial(_ffn_kernel, k_gate=k_gate, tm=tm, n_lat=n_lat, has_ctx=rows > n_lat)
    row_spec = pl.BlockSpec((None, tm, d), lambda b, j, f: (b, j, 0))
    return pl.pallas_call(
        kern,
        grid=(n_batch, rows // tm, nf),
        in_specs=[row_spec,
                  pl.BlockSpec((d, tf), lambda b, j, f: (0, f)),
                  pl.BlockSpec((d, tf), lambda b, j, f: (0, nf + f)),
                  pl.BlockSpec((tf, d), lambda b, j, f: (f, 0)),
                  row_spec,
                  pl.BlockSpec((1, d), lambda b, j, f: (0, 0)),
                  pl.BlockSpec((None, 6, 1, d), lambda b, j, f: (b, 0, 0, 0)),
                  pl.BlockSpec((None, 6, 1, d), lambda b, j, f: (n_batch, 0, 0, 0))],
        out_specs=row_spec,
        out_shape=jax.ShapeDtypeStruct((n_batch, rows, d), F32),
        scratch_shapes=[pltpu.VMEM((tm, d), F32)],
        compiler_params=_params(3),
    )(h, w_in, w_in, w_out, x, g.reshape(1, d), mods, mods)


def _router_kernel(h_ref, w_ref, idx_ref, wt_ref, *, n_experts):
    logits = jnp.dot(h_ref[...].astype(BF16), w_ref[...], preferred_element_type=F32)
    lane = lax.broadcasted_iota(jnp.int32, logits.shape, 1)
    neg = jnp.float32(-jnp.inf)
    lg = jnp.where(lane < n_experts, logits, neg)
    m1 = jnp.max(lg, axis=-1, keepdims=True)
    i1 = jnp.min(jnp.where(lg == m1, lane, LANES), axis=-1, keepdims=True)
    lg2 = jnp.where(lane == i1, neg, lg)
    m2 = jnp.max(lg2, axis=-1, keepdims=True)
    i2 = jnp.min(jnp.where(lg2 == m2, lane, LANES), axis=-1, keepdims=True)
    e2 = jnp.exp(m2 - m1)
    den = 1.0 + e2
    idx_ref[...] = jnp.where(lane == 0, i1, jnp.where(lane == 1, i2, 0))
    wt_ref[...] = jnp.where(lane == 0, 1.0 / den, jnp.where(lane == 1, e2 / den, 0.0))


def _router(h, w_router):
    n_tok, d = h.shape
    n_experts = w_router.shape[1]
    tm = _pick(n_tok, 1024, 8)
    w_pad = jnp.zeros((d, LANES), BF16).at[:, :n_experts].set(w_router.astype(BF16))
    out_spec = pl.BlockSpec((tm, LANES), lambda i: (i, 0))
    return pl.pallas_call(
        functools.partial(_router_kernel, n_experts=n_experts),
        grid=(n_tok // tm,),
        in_specs=[pl.BlockSpec((tm, d), lambda i: (i, 0)), pl.BlockSpec((d, LANES), lambda i: (0, 0))],
        out_specs=[out_spec, out_spec],
        out_shape=[jax.ShapeDtypeStruct((n_tok, LANES), jnp.int32), jax.ShapeDtypeStruct((n_tok, LANES), F32)],
        compiler_params=_params(1),
    )(h, w_pad)


def _route_plan(top_idx, n_experts, tm, n_tiles):
    n_tok = top_idx.shape[0]
    e_flat = top_idx.reshape(-1)
    onehot = (e_flat[:, None] == jnp.arange(n_experts, dtype=jnp.int32)[None, :]).astype(jnp.int32)
    csum = jnp.cumsum(onehot, axis=0)
    rank = jnp.sum((csum - onehot) * onehot, axis=1)
    tiles_e = (csum[-1] + tm - 1) // tm
    tiles_end = jnp.cumsum(tiles_e)
    pos = (tiles_end - tiles_e)[e_flat] * tm + rank
    n_used = tiles_end[-1]
    tile_ids = jnp.arange(n_tiles, dtype=jnp.int32)
    tile_expert = jnp.sum((tile_ids[:, None] >= tiles_end[None, :]).astype(jnp.int32), axis=1)
    tile_expert = jnp.minimum(tile_expert, n_experts - 1)
    last_expert = tile_expert[jnp.maximum(n_used - 1, 0)]
    tile_expert = jnp.where(tile_ids < n_used, tile_expert, last_expert)
    row_token = jnp.zeros((n_tiles * tm,), jnp.int32).at[pos].set(jnp.arange(TOP_K * n_tok, dtype=jnp.int32) // TOP_K)
    return row_token, pos.reshape(n_tok, TOP_K).astype(jnp.int32), tile_expert, n_used.reshape(1).astype(jnp.int32)


def _gather_kernel(tok_ref, src_ref, o_ref, buf_ref, sem, *, tg):
    def issue(r, carry):
        pltpu.make_async_copy(src_ref.at[pl.ds(tok_ref[0, r], 1)], buf_ref.at[pl.ds(r, 1)], sem).start()
        return carry

    def drain(r, carry):
        pltpu.make_async_copy(src_ref.at[pl.ds(0, 1)], buf_ref.at[pl.ds(r, 1)], sem).wait()
        return carry

    lax.fori_loop(0, tg, issue, 0, unroll=8)
    lax.fori_loop(0, tg, drain, 0, unroll=8)
    o_ref[...] = buf_ref[...].astype(o_ref.dtype)


def _gather_rows(src, row_token, tg, out_dtype):
    n_rows = row_token.shape[0]
    d = src.shape[1]
    return pl.pallas_call(
        functools.partial(_gather_kernel, tg=tg),
        grid=(n_rows // tg,),
        in_specs=[pl.BlockSpec((None, 1, tg), lambda i: (i, 0, 0), memory_space=pltpu.SMEM),
                  pl.BlockSpec(memory_space=pl.ANY)],
        out_specs=pl.BlockSpec((tg, d), lambda i: (i, 0)),
        out_shape=jax.ShapeDtypeStruct((n_rows, d), out_dtype),
        scratch_shapes=[pltpu.VMEM((tg, d), src.dtype), pltpu.SemaphoreType.DMA(())],
        compiler_params=_params(1),
    )(row_token.reshape(n_rows // tg, 1, tg), src)


def _moe_ffn_kernel(te_ref, nv_ref, xs_ref, wg_ref, wu_ref, wo_ref, o_ref):
    i, f = pl.program_id(0), pl.program_id(1)
    valid = i < nv_ref[0]

    @pl.when(f == 0)
    def _():
        o_ref[...] = jnp.zeros_like(o_ref)

    @pl.when(valid)
    def _():
        xb = xs_ref[...]
        gate = jnp.dot(xb, wg_ref[...], preferred_element_type=F32)
        up = jnp.dot(xb, wu_ref[...], preferred_element_type=F32)
        o_ref[...] += jnp.dot((_silu(gate) * up).astype(BF16), wo_ref[...], preferred_element_type=F32)


def _moe_ffn(xs, w_in, w_out, tile_expert, n_used, tm):
    n_rows, d = xs.shape
    ff = w_out.shape[1]
    tf = _pick(ff, 512, LANES)
    nf = ff // tf

    def f_eff(i, f, nv):
        return jnp.where(i < nv[0], f, nf - 1)

    grid_spec = pltpu.PrefetchScalarGridSpec(
        num_scalar_prefetch=2,
        grid=(n_rows // tm, nf),
        in_specs=[pl.BlockSpec((tm, d), lambda i, f, te, nv: (jnp.minimum(i, jnp.maximum(nv[0] - 1, 0)), 0)),
                  pl.BlockSpec((None, d, tf), lambda i, f, te, nv: (te[i], 0, f_eff(i, f, nv))),
                  pl.BlockSpec((None, d, tf), lambda i, f, te, nv: (te[i], 0, nf + f_eff(i, f, nv))),
                  pl.BlockSpec((None, tf, d), lambda i, f, te, nv: (te[i], f_eff(i, f, nv), 0))],
        out_specs=pl.BlockSpec((tm, d), lambda i, f, te, nv: (i, 0)))
    return pl.pallas_call(
        _moe_ffn_kernel, grid_spec=grid_spec,
        out_shape=jax.ShapeDtypeStruct((n_rows, d), F32),
        compiler_params=_params(2),
    )(tile_expert, n_used, xs, w_in, w_in, w_out)


def _combine_kernel(pos_ref, ys_ref, wt_ref, x_ref, g_ref, ml_ref, mc_ref, o_ref, buf_ref, sem, *,
                    k_gate, tc, n_lat, has_ctx):
    def issue(r, carry):
        for k in range(TOP_K):
            pltpu.make_async_copy(ys_ref.at[pl.ds(pos_ref[0, TOP_K * r + k], 1)],
                                  buf_ref.at[k, pl.ds(r, 1)], sem).start()
        return carry

    def drain(r, carry):
        for k in range(TOP_K):
            pltpu.make_async_copy(ys_ref.at[pl.ds(0, 1)], buf_ref.at[k, pl.ds(r, 1)], sem).wait()
        return carry

    lax.fori_loop(0, tc, issue, 0)
    lax.fori_loop(0, tc, drain, 0)
    wt = wt_ref[...]
    f = wt[:, 0:1] * buf_ref[0]
    for k in range(1, TOP_K):
        f = f + wt[:, k:k + 1] * buf_ref[k]
    o_ref[...] = _gated_residual(x_ref[...], f, g_ref, ml_ref, mc_ref, k_gate, pl.program_id(1), tc, n_lat, has_ctx)


def _combine(ys, pos, wts, x, g, mods, rows, n_lat, k_gate):
    n_batch, _, d = x.shape
    tc = _pick(rows, 256, 8)
    nj = rows // tc
    kern = functools.partial(_combine_kernel, k_gate=k_gate, tc=tc, n_lat=n_lat, has_ctx=rows > n_lat)
    row_spec = pl.BlockSpec((None, tc, d), lambda b, j: (b, j, 0))
    return pl.pallas_call(
        kern,
        grid=(n_batch, nj),
        in_specs=[pl.BlockSpec((None, 1, TOP_K * tc), lambda b, j: (b * nj + j, 0, 0), memory_space=pltpu.SMEM),
                  pl.BlockSpec(memory_space=pl.ANY),
                  pl.BlockSpec((tc, LANES), lambda b, j: (b * nj + j, 0)),
                  row_spec,
                  pl.BlockSpec((1, d), lambda b, j: (0, 0)),
                  pl.BlockSpec((None, 6, 1, d), lambda b, j: (b, 0, 0, 0)),
                  pl.BlockSpec((None, 6, 1, d), lambda b, j: (n_batch, 0, 0, 0))],
        out_specs=row_spec,
        out_shape=jax.ShapeDtypeStruct((n_batch, rows, d), F32),
        scratch_shapes=[pltpu.VMEM((TOP_K, tc, d), F32), pltpu.SemaphoreType.DMA(())],
        compiler_params=_params(2),
    )(pos.reshape(n_batch * nj, 1, TOP_K * tc), ys, wts, x, g.reshape(1, d), mods, mods)


def _ffn_moe(h, w_router, w_in, w_out, x, g, mods, rows, n_lat, k_gate):
    n_batch, _, d = h.shape
    n_tok = n_batch * rows
    n_experts = w_router.shape[1]
    tm = min(768, max(16, (TOP_K * n_tok // n_experts) // 16 * 16))
    n_tiles = (TOP_K * n_tok + n_experts * (tm - 1)) // tm
    h_flat = h.reshape(n_tok, d)
    idx, wts = _router(h_flat, w_router)
    row_token, pos, tile_expert, n_used = _route_plan(idx[:, :TOP_K], n_experts, tm, n_tiles)
    xs = _gather_rows(h_flat, row_token, tm, BF16)
    ys = _moe_ffn(xs, w_in, w_out, tile_expert, n_used, tm)
    return _combine(ys, pos, wts, x, g, mods, rows, n_lat, k_gate)


def kernel(x, c, ctx, c_ctx, w_mod, b_mod, norm_g, w_in, qk_norm_g, diff_lambda, diff_subln_g, ret_log_decay,
           w_branch, w_branch_gate, w_out, ffn_w_in, ffn_w_out, moe_router, moe_w_in, moe_w_out):
    n_batch, n_lat, d = x.shape
    n_ctx = ctx.shape[1]
    depth = w_mod.shape[0]
    a_heads, b_heads, c_heads = d // 256, d // 512, d // 256
    kinds, off = _proj_layout(d)

    mod_rows = (n_batch + 1 + 7) // 8 * 8
    cond = jnp.zeros((mod_rows, d), F32).at[:n_batch].set(c).at[n_batch].set(c_ctx)
    mods_all = _modvec(cond, w_mod, b_mod).reshape(depth, mod_rows, 6, 1, d)
    tabs = _rope_tables(n_lat, n_ctx)

    xs = jnp.concatenate([x, ctx], axis=1)
    n_all = n_lat + n_ctx
    for i in range(depth):
        rows = n_all if i < depth - 1 else n_lat
        lam_init = 0.8 - 0.6 * math.exp(-0.3 * i)
        mods = mods_all[i]
        h = _norm_mod(xs, norm_g[i, 0], mods, n_all, n_lat, 0, 1, BF16)
        p = _proj(h, w_in[i].astype(BF16), tabs, qk_norm_g[i], kinds)
        o_a = _attention(p, diff_lambda[i], diff_subln_g[i], rows, n_lat, n_ctx, a_heads,
                         off["aq"], off["ak"], off["av"], 1, True, lam_init)
        o_b = _retention(p, ret_log_decay[i], rows, n_lat, n_ctx, b_heads, off)
        o_c = _attention(p, diff_lambda[i], diff_subln_g[i], rows, n_lat, n_ctx, c_heads,
                         off["cq"], off["ck"], off["cv"], C_GROUP, False, lam_init)
        y = _merge(h, (o_a, o_b, o_c), w_branch_gate[i].astype(BF16), w_branch[i].astype(BF16), rows)
        xs = _out_resid(y, w_out[i].astype(BF16), xs, norm_g[i, 1], mods, rows, n_lat, 2)
        j = i // 2
        if i % 2 == 0:
            h2 = _norm_mod(xs, norm_g[i, 2], mods, rows, n_lat, 3, 4, BF16)
            xs = _ffn_dense(h2, ffn_w_in[j].astype(BF16), ffn_w_out[j].astype(BF16), xs, norm_g[i, 3], mods,
                            rows, n_lat, 5)
        else:
            h2 = _norm_mod(xs, norm_g[i, 2], mods, rows, n_lat, 3, 4, F32)
            xs = _ffn_moe(h2, moe_router[j], moe_w_in[j].astype(BF16), moe_w_out[j].astype(BF16), xs,
                          norm_g[i, 3], mods, rows, n_lat, 5)
    return xs[:, :n_lat]
```

```python
import functools
import math

import jax
import jax.numpy as jnp
from jax import lax
from jax.experimental import pallas as pl
from jax.experimental.pallas import tpu as pltpu

F32 = jnp.float32
BF16 = jnp.bfloat16

GRID_W = 64
ROPE_THETA = 10000.0
EPS = 1e-6
GN_EPS = 1e-5
A_DIM = 64
B_QK = 128
B_V = 256
C_DIM = 128
C_GROUP = 4
TOP_K = 2
LANES = 128
SUBLANES = 8
LOG2E = 1.4426950408889634
BOUND_SLACK = 1.02
DEN_FLOOR = 2.0 ** -88
VMEM_LIMIT_BYTES = 56 * 1024 * 1024


def _pick(n, target, mult):
    best = None
    for t in range(mult, min(n, target) + 1, mult):
        if n % t == 0:
            best = t
    return best if best is not None else n


def _params(n_axes):
    return pltpu.CompilerParams(dimension_semantics=("arbitrary",) * n_axes,
                                vmem_limit_bytes=VMEM_LIMIT_BYTES)


def _silu(v):
    return v * jax.nn.sigmoid(v)


def _rms(v, eps):
    return v * lax.rsqrt(jnp.mean(v * v, axis=-1, keepdims=True) + eps)


def _row_is_ctx(tile_idx, tm, n_lat):
    row = tile_idx * tm + lax.broadcasted_iota(jnp.int32, (tm, 1), 0)
    return row >= n_lat


def _modvec_kernel(c_ref, w_ref, b_ref, o_ref):
    a = _silu(c_ref[...]).astype(BF16)
    o_ref[...] = jnp.dot(a, w_ref[...].astype(BF16), preferred_element_type=F32) + b_ref[...]


def _modvec(cond, w_mod, b_mod):
    depth, d, n6 = w_mod.shape
    rows = cond.shape[0]
    tn = _pick(n6, 1024, LANES)
    return pl.pallas_call(
        _modvec_kernel,
        grid=(depth, n6 // tn),
        in_specs=[pl.BlockSpec((rows, d), lambda i, n: (0, 0)),
                  pl.BlockSpec((None, d, tn), lambda i, n: (i, 0, n)),
                  pl.BlockSpec((None, 1, tn), lambda i, n: (i, 0, n))],
        out_specs=pl.BlockSpec((None, rows, tn), lambda i, n: (i, 0, n)),
        out_shape=jax.ShapeDtypeStruct((depth, rows, n6), F32),
        compiler_params=_params(2),
    )(cond, w_mod, b_mod.reshape(depth, 1, n6))


def _norm_mod_kernel(x_ref, g_ref, ml_ref, mc_ref, o_ref, *, k_shift, k_scale, n_lat, tm, has_ctx):
    y = _rms(x_ref[...], EPS) * g_ref[...]
    shift, scale = ml_ref[k_shift], ml_ref[k_scale]
    if has_ctx:
        is_ctx = _row_is_ctx(pl.program_id(1), tm, n_lat)
        shift = jnp.where(is_ctx, mc_ref[k_shift], shift)
        scale = jnp.where(is_ctx, mc_ref[k_scale], scale)
    o_ref[...] = (y * (1.0 + scale) + shift).astype(o_ref.dtype)


def _norm_mod(x, g, mods, rows, n_lat, k_shift, k_scale, out_dtype):
    n_batch, _, d = x.shape
    tm = _pick(rows, 512, 16)
    kern = functools.partial(_norm_mod_kernel, k_shift=k_shift, k_scale=k_scale, n_lat=n_lat, tm=tm,
                             has_ctx=rows > n_lat)
    return pl.pallas_call(
        kern,
        grid=(n_batch, rows // tm),
        in_specs=[pl.BlockSpec((None, tm, d), lambda b, j: (b, j, 0)),
                  pl.BlockSpec((1, d), lambda b, j: (0, 0)),
                  pl.BlockSpec((None, 6, 1, d), lambda b, j: (b, 0, 0, 0)),
                  pl.BlockSpec((None, 6, 1, d), lambda b, j: (n_batch, 0, 0, 0))],
        out_specs=pl.BlockSpec((None, tm, d), lambda b, j: (b, j, 0)),
        out_shape=jax.ShapeDtypeStruct((n_batch, rows, d), out_dtype),
        compiler_params=_params(2),
    )(x, g.reshape(1, d), mods, mods)


def _rope_tables(n_lat, n_ctx):
    rows = n_lat // GRID_W
    row = jnp.repeat(jnp.arange(rows, dtype=F32), GRID_W)
    col = jnp.tile(jnp.arange(GRID_W, dtype=F32), rows)

    def angles(dim):
        n = dim // 4
        inv = ROPE_THETA ** (-jnp.arange(n, dtype=F32) / n)
        return jnp.concatenate([row[:, None] * inv, col[:, None] * inv], axis=-1)

    a64, a128 = angles(A_DIM), angles(C_DIM)
    lane = jnp.arange(LANES)
    c64, s64 = jnp.cos(a64)[:, lane % 32], jnp.sin(a64)[:, lane % 32]
    first_half = (lane % 64) < 32
    sa64 = jnp.where(first_half, -s64, 0.0)
    sb64 = jnp.where(first_half, 0.0, s64)
    c128 = jnp.cos(a128)[:, lane % 64]
    s128 = jnp.sin(a128)[:, lane % 64] * jnp.where(lane < 64, -1.0, 1.0)
    lat = jnp.stack([c64, sa64, sb64, c128, s128])
    ident = jnp.zeros((5, n_ctx, LANES), F32).at[jnp.array([0, 3])].set(1.0)
    return jnp.concatenate([lat, ident], axis=1)


def _proj_epilogue(kind, x, tab_ref, qkg_ref):
    name = kind[0]
    if name == "none":
        return x
    if name == "silu":
        return _silu(x)
    if name == "rope64":
        y = (x * tab_ref[0] + pltpu.roll(x, 96, 1) * tab_ref[1] + pltpu.roll(x, 32, 1) * tab_ref[2])
        return y * kind[1]
    if name == "norm_rope128":
        x = _rms(x, EPS) * qkg_ref[kind[2]:kind[2] + 1, :]
    y = x * tab_ref[3] + pltpu.roll(x, 64, 1) * tab_ref[4]
    return y * kind[1]


def _proj_kernel(h_ref, w_ref, tab_ref, qkg_ref, o_ref, *, groups):
    n = pl.program_id(2)
    acc = jnp.dot(h_ref[...], w_ref[...], preferred_element_type=F32)
    for tiles, kinds in groups:
        cond = functools.reduce(jnp.logical_or, [n == t for t in tiles])

        @pl.when(cond)
        def _(kinds=kinds):
            for c, kind in enumerate(kinds):
                sl = slice(c * LANES, (c + 1) * LANES)
                o_ref[:, sl] = _proj_epilogue(kind, acc[:, sl], tab_ref, qkg_ref).astype(o_ref.dtype)


def _proj_layout(d):
    a_heads, b_heads, c_heads = d // 256, d // 512, d // 256
    c_kv = c_heads // C_GROUP
    segs = [("aq", a_heads, ("rope64", A_DIM ** -0.5 * LOG2E)), ("ak", a_heads, ("rope64", 1.0)),
            ("av", a_heads, ("none",)),
            ("bq", b_heads, ("rope128", 1.0)), ("bk", b_heads, ("rope128", B_QK ** -0.5)),
            ("bv", 2 * b_heads, ("none",)), ("bg", 2 * b_heads, ("silu",)),
            ("cq", c_heads, ("norm_rope128", C_DIM ** -0.5 * LOG2E, 0)), ("ck", c_kv, ("norm_rope128", 1.0, 1)),
            ("cv", c_kv, ("none",))]
    kinds, off = [], {}
    for name, nblk, kind in segs:
        off[name] = len(kinds)
        kinds += [kind] * nblk
    return kinds, off


def _proj(h, w, tabs, qkg, kinds):
    n_batch, rows, d = h.shape
    width = w.shape[1]
    tm = _pick(rows, 1152, 16)
    tn = _pick(width, 512, LANES)
    per = tn // LANES
    by_kinds = {}
    for t in range(width // tn):
        by_kinds.setdefault(tuple(kinds[t * per:(t + 1) * per]), []).append(t)
    groups = tuple((tuple(tiles), ks) for ks, tiles in by_kinds.items())
    return pl.pallas_call(
        functools.partial(_proj_kernel, groups=groups),
        grid=(n_batch, rows // tm, width // tn),
        in_specs=[pl.BlockSpec((None, tm, d), lambda b, j, n: (b, j, 0)),
                  pl.BlockSpec((d, tn), lambda b, j, n: (0, n)),
                  pl.BlockSpec((5, tm, LANES), lambda b, j, n: (0, j, 0)),
                  pl.BlockSpec((2, LANES), lambda b, j, n: (0, 0))],
        out_specs=pl.BlockSpec((None, tm, tn), lambda b, j, n: (b, j, n)),
        out_shape=jax.ShapeDtypeStruct((n_batch, rows, width), BF16),
        compiler_params=_params(3),
    )(h, w, tabs, qkg)


def _scores(q, k):
    return lax.dot_general(q, k, (((1,), (1,)), ((), ())), preferred_element_type=F32)


def _softmax_pv_rowmax(q, k, vx):
    s = _scores(q, k)
    e = jnp.exp2(s - jnp.max(s, axis=-1, keepdims=True)).astype(BF16)
    r = jnp.dot(e, vx, preferred_element_type=F32)
    return r[:, :LANES] / r[:, LANES:LANES + 1]


def _softmax_pv_bound(q, k, vx, ksq_max):
    qf = q.astype(F32)
    shift = BOUND_SLACK * jnp.sqrt(jnp.sum(qf * qf, axis=-1, keepdims=True) * ksq_max)
    e = jnp.exp2(_scores(q, k) - shift).astype(BF16)
    r = jnp.dot(e, vx, preferred_element_type=F32)
    return r[:, :LANES], r[:, LANES:LANES + 1]


def _attn_kernel(q_ref, k_ref, v_ref, lam_ref, g_ref, o_ref, vx_ref, ksq_ref, *,
                 diff, lam_init, n_lat_tiles, n_lat, has_ctx):
    i = pl.program_id(2)

    @pl.when(i == 0)
    def _():
        vx_ref[:, :LANES] = v_ref[...]
        vx_ref[:, LANES:] = jnp.ones((vx_ref.shape[0], LANES), BF16)
        k = k_ref[...]
        ksq = jnp.dot(k * k, jnp.ones((LANES, LANES), BF16), preferred_element_type=F32)
        ksq_ref[...] = jnp.broadcast_to(jnp.max(ksq, axis=0, keepdims=True), ksq_ref.shape)

    def run(k, vx, nq):
        q = q_ref[:nq, :]
        if diff:
            qf = q.astype(F32)
            lane = lax.broadcasted_iota(jnp.int32, qf.shape, 1)
            qs = (jnp.where(lane < A_DIM, qf, 0.0).astype(BF16), jnp.where(lane >= A_DIM, qf, 0.0).astype(BF16))
        else:
            qs = (q,)

        def finish(outs):
            if diff:
                lv = lam_ref[...]
                lam = (jnp.exp(jnp.sum(lv[0:1] * lv[1:2], axis=-1, keepdims=True))
                       - jnp.exp(jnp.sum(lv[2:3] * lv[3:4], axis=-1, keepdims=True)) + lam_init)
                o = _rms(outs[0] - lam * outs[1], EPS) * g_ref[...] * (1.0 - lam_init)
            else:
                o = outs[0]
            o_ref[:nq, :] = o.astype(o_ref.dtype)

        parts = [_softmax_pv_bound(qq, k, vx, ksq_ref[0:1, 0:1]) for qq in qs]
        den_min = functools.reduce(jnp.minimum, [den for _, den in parts])
        precise = jnp.min(den_min) >= DEN_FLOOR

        @pl.when(precise)
        def _():
            finish([num / den for num, den in parts])

        @pl.when(jnp.logical_not(precise))
        def _():
            finish([_softmax_pv_rowmax(qq, k, vx) for qq in qs])

    tq = q_ref.shape[0]
    if not has_ctx:
        run(k_ref[...], vx_ref[...], tq)
        return

    @pl.when(i < n_lat_tiles)
    def _():
        run(k_ref[...], vx_ref[...], tq)

    @pl.when(i >= n_lat_tiles)
    def _():
        run(k_ref[n_lat:, :], vx_ref[n_lat:, :], min(tq, vx_ref.shape[0] - n_lat))


def _query_tile(n_lat, n_ctx, has_ctx):
    tq = _pick(n_lat, 512, 16)
    assert not has_ctx or n_ctx % tq == 0 or n_ctx < tq
    return tq


def _attention(p, lam_vec, subln_g, rows, n_lat, n_ctx, n_heads, q_off, k_off, v_off, kv_group, diff, lam_init):
    n_batch, n_keys, _ = p.shape
    has_ctx = rows > n_lat
    tq = _query_tile(n_lat, n_ctx, has_ctx)
    kern = functools.partial(_attn_kernel, diff=diff, lam_init=lam_init, n_lat_tiles=n_lat // tq,
                             n_lat=n_lat, has_ctx=has_ctx)
    return pl.pallas_call(
        kern,
        grid=(n_batch, n_heads, pl.cdiv(rows, tq)),
        in_specs=[pl.BlockSpec((None, tq, LANES), lambda b, h, i: (b, i, q_off + h)),
                  pl.BlockSpec((None, n_keys, LANES), lambda b, h, i: (b, 0, k_off + h // kv_group)),
                  pl.BlockSpec((None, n_keys, LANES), lambda b, h, i: (b, 0, v_off + h // kv_group)),
                  pl.BlockSpec(lam_vec.shape, lambda b, h, i: (0, 0)),
                  pl.BlockSpec((1, LANES), lambda b, h, i: (0, 0))],
        out_specs=pl.BlockSpec((None, tq, LANES), lambda b, h, i: (b, i, h)),
        out_shape=jax.ShapeDtypeStruct((n_batch, rows, n_heads * LANES), BF16),
        scratch_shapes=[pltpu.VMEM((n_keys, 2 * LANES), BF16), pltpu.VMEM((SUBLANES, LANES), F32)],
        compiler_params=_params(3),
    )(p, p, p, lam_vec, subln_g.reshape(1, LANES))


def _ret_decay(lgf, lgb, t, pf, pb):
    df = (t - pf).astype(F32)
    db = (pb - t).astype(F32)
    fwd = jnp.where(df >= 0, jnp.exp(lgf * jnp.maximum(df, 0.0)), 0.0)
    bwd = jnp.where(db >= 0, jnp.exp(lgb * jnp.maximum(db, 0.0)), 0.0)
    return fwd + bwd


def _ret_kernel(ld_ref, q_ref, k_ref, v_ref, g_ref, o_ref, d_ref, *, tq, n_keys, n_lat, n_heads, has_ctx):
    h, i, b = pl.program_id(0), pl.program_id(1), pl.program_id(2)
    n_ctx = n_keys - n_lat
    n_lat_tiles = n_lat // tq
    lgf, lgb = ld_ref[h], ld_ref[n_heads + h]
    t_loc = lax.broadcasted_iota(jnp.int32, (tq, 1), 0)

    def lat_decay():
        col = lax.broadcasted_iota(jnp.int32, (1, n_keys), 1)
        pf = jnp.where(col < n_lat, col, col - n_keys)
        d_ref[...] = _ret_decay(lgf, lgb, i * tq + t_loc, pf, col)

    def ctx_decay():
        col = lax.broadcasted_iota(jnp.int32, (1, n_ctx), 1)
        d_ref[:, :n_ctx] = _ret_decay(lgf, lgb, (i - n_lat_tiles) * tq + t_loc, col, col)

    def run(k, v, decay, nq):
        s = lax.dot_general(q_ref[:nq, :], k, (((1,), (1,)), ((), ())), preferred_element_type=F32)
        o = jnp.dot((s * decay).astype(BF16), v, preferred_element_type=F32)
        mu = jnp.mean(o, axis=-1, keepdims=True)
        var = jnp.mean(jnp.square(o - mu), axis=-1, keepdims=True)
        o_ref[:nq, :] = ((o - mu) * lax.rsqrt(var + GN_EPS) * g_ref[:nq, :].astype(F32)).astype(o_ref.dtype)

    if not has_ctx:
        pl.when(b == 0)(lat_decay)
        run(k_ref[...], v_ref[...], d_ref[...], tq)
        return

    pl.when((b == 0) & (i < n_lat_tiles))(lat_decay)
    pl.when((b == 0) & (i >= n_lat_tiles))(ctx_decay)

    @pl.when(i < n_lat_tiles)
    def _():
        run(k_ref[...], v_ref[...], d_ref[...], tq)

    @pl.when(i >= n_lat_tiles)
    def _():
        cq = min(tq, n_ctx)
        run(k_ref[n_lat:, :], v_ref[n_lat:, :], d_ref[:cq, :n_ctx], cq)


def _retention(p, log_decay, rows, n_lat, n_ctx, n_heads, off):
    n_batch, n_keys, _ = p.shape
    has_ctx = rows > n_lat
    tq = _query_tile(n_lat, n_ctx, has_ctx)
    kern = functools.partial(_ret_kernel, tq=tq, n_keys=n_keys, n_lat=n_lat, n_heads=n_heads, has_ctx=has_ctx)
    v_off, g_off = off["bv"] // 2, off["bg"] // 2
    grid_spec = pltpu.PrefetchScalarGridSpec(
        num_scalar_prefetch=1,
        grid=(n_heads, pl.cdiv(rows, tq), n_batch),
        in_specs=[pl.BlockSpec((None, tq, B_QK), lambda h, i, b, ld: (b, i, off["bq"] + h)),
                  pl.BlockSpec((None, n_keys, B_QK), lambda h, i, b, ld: (b, 0, off["bk"] + h)),
                  pl.BlockSpec((None, n_keys, B_V), lambda h, i, b, ld: (b, 0, v_off + h)),
                  pl.BlockSpec((None, tq, B_V), lambda h, i, b, ld: (b, i, g_off + h))],
        out_specs=pl.BlockSpec((None, tq, B_V), lambda h, i, b, ld: (b, i, h)),
        scratch_shapes=[pltpu.VMEM((tq, n_keys), F32)])
    return pl.pallas_call(
        kern, grid_spec=grid_spec,
        out_shape=jax.ShapeDtypeStruct((n_batch, rows, n_heads * B_V), BF16),
        compiler_params=_params(3),
    )(log_decay.reshape(-1).astype(F32), p, p, p, p)


def _merge_kernel(h_ref, oa_ref, ob_ref, oc_ref, ga_ref, gb_ref, gc_ref, wb_ref, y_ref):
    h = h_ref[...]
    y = None
    for idx, (o_ref, wg_ref) in enumerate(((oa_ref, ga_ref), (ob_ref, gb_ref), (oc_ref, gc_ref))):
        gate = jax.nn.sigmoid(jnp.dot(h, wg_ref[...], preferred_element_type=F32))
        term = gate * jnp.dot(o_ref[...], wb_ref[idx], preferred_element_type=F32)
        y = term if y is None else y + term
    y_ref[...] = y.astype(y_ref.dtype)


def _merge(h, outs, w_gate, w_branch, rows):
    n_batch, _, d = h.shape
    bw = w_branch.shape[1]
    tm = _pick(rows, 1152, 16)
    tn = _pick(d, 512, LANES)
    nn = d // tn
    o_spec = pl.BlockSpec((None, tm, bw), lambda b, j, n: (b, j, 0))
    g_specs = [pl.BlockSpec((d, tn), functools.partial(lambda b, j, n, k: (0, k * nn + n), k=k)) for k in range(3)]
    return pl.pallas_call(
        _merge_kernel,
        grid=(n_batch, rows // tm, nn),
        in_specs=[pl.BlockSpec((None, tm, d), lambda b, j, n: (b, j, 0)), o_spec, o_spec, o_spec,
                  *g_specs, pl.BlockSpec((3, bw, tn), lambda b, j, n: (0, 0, n))],
        out_specs=pl.BlockSpec((None, tm, tn), lambda b, j, n: (b, j, n)),
        out_shape=jax.ShapeDtypeStruct((n_batch, rows, d), BF16),
        compiler_params=_params(3),
    )(h, *outs, w_gate, w_gate, w_gate, w_branch)


def _gated_residual(x, f, g_ref, ml_ref, mc_ref, k_gate, tile_idx, tm, n_lat, has_ctx):
    gate = ml_ref[k_gate]
    if has_ctx:
        gate = jnp.where(_row_is_ctx(tile_idx, tm, n_lat), mc_ref[k_gate], gate)
    return x + gate * (_rms(f, EPS) * g_ref[...])


def _out_resid_kernel(y_ref, w_ref, x_ref, g_ref, ml_ref, mc_ref, o_ref, *, k_gate, tm, n_lat, has_ctx):
    f = jnp.dot(y_ref[...], w_ref[...], preferred_element_type=F32)
    o_ref[...] = _gated_residual(x_ref[...], f, g_ref, ml_ref, mc_ref, k_gate, pl.program_id(1), tm, n_lat, has_ctx)


def _out_resid(y, w_out, x, g, mods, rows, n_lat, k_gate):
    n_batch, _, d = y.shape
    tm = _pick(rows, 512, 16)
    kern = functools.partial(_out_resid_kernel, k_gate=k_gate, tm=tm, n_lat=n_lat, has_ctx=rows > n_lat)
    row_spec = pl.BlockSpec((None, tm, d), lambda b, j: (b, j, 0))
    return pl.pallas_call(
        kern,
        grid=(n_batch, rows // tm),
        in_specs=[row_spec, pl.BlockSpec((d, d), lambda b, j: (0, 0)), row_spec,
                  pl.BlockSpec((1, d), lambda b, j: (0, 0)),
                  pl.BlockSpec((None, 6, 1, d), lambda b, j: (b, 0, 0, 0)),
                  pl.BlockSpec((None, 6, 1, d), lambda b, j: (n_batch, 0, 0, 0))],
        out_specs=row_spec,
        out_shape=jax.ShapeDtypeStruct((n_batch, rows, d), F32),
        compiler_params=_params(2),
    )(y, w_out, x, g.reshape(1, d), mods, mods)


def _ffn_kernel(h_ref, wg_ref, wu_ref, wo_ref, x_ref, g_ref, ml_ref, mc_ref, o_ref, acc_ref, *,
                k_gate, tm, n_lat, has_ctx):
    f = pl.program_id(2)
    h = h_ref[...]
    gate = jnp.dot(h, wg_ref[...], preferred_element_type=F32)
    up = jnp.dot(h, wu_ref[...], preferred_element_type=F32)
    part = jnp.dot((_silu(gate) * up).astype(BF16), wo_ref[...], preferred_element_type=F32)

    @pl.when(f == 0)
    def _():
        acc_ref[...] = part

    @pl.when(f > 0)
    def _():
        acc_ref[...] += part

    @pl.when(f == pl.num_programs(2) - 1)
    def _():
        o_ref[...] = _gated_residual(x_ref[...], acc_ref[...], g_ref, ml_ref, mc_ref, k_gate,
                                     pl.program_id(1), tm, n_lat, has_ctx)


def _ffn_dense(h, w_in, w_out, x, g, mods, rows, n_lat, k_gate):
    n_batch, _, d = h.shape
    ff = w_out.shape[0]
    tm = _pick(rows, 576, 16)
    tf = _pick(ff, 512, LANES)
    nf = ff // tf
    kern = functools.partial(_ffn_kernel, k_gate=k_gate, tm=tm, n_lat=n_lat, has_ctx=rows > n_lat)
    row_spec = pl.BlockSpec((None, tm, d), lambda b, j, f: (b, j, 0))
    return pl.pallas_call(
        kern,
        grid=(n_batch, rows // tm, nf),
        in_specs=[row_spec,
                  pl.BlockSpec((d, tf), lambda b, j, f: (0, f)),
                  pl.BlockSpec((d, tf), lambda b, j, f: (0, nf + f)),
                  pl.BlockSpec((tf, d), lambda b, j, f: (f, 0)),
                  row_spec,
                  pl.BlockSpec((1, d), lambda b, j, f: (0, 0)),
                  pl.BlockSpec((None, 6, 1, d), lambda b, j, f: (b, 0, 0, 0)),
                  pl.BlockSpec((None, 6, 1, d), lambda b, j, f: (n_batch, 0, 0, 0))],
        out_specs=row_spec,
        out_shape=jax.ShapeDtypeStruct((n_batch, rows, d), F32),
        scratch_shapes=[pltpu.VMEM((tm, d), F32)],
        compiler_params=_params(3),
    )(h, w_in, w_in, w_out, x, g.reshape(1, d), mods, mods)


def _router_kernel(h_ref, w_ref, idx_ref, wt_ref, *, n_experts):
    logits = jnp.dot(h_ref[...].astype(BF16), w_ref[...], preferred_element_type=F32)
    lane = lax.broadcasted_iota(jnp.int32, logits.shape, 1)
    neg = jnp.float32(-jnp.inf)
    lg = jnp.where(lane < n_experts, logits, neg)
    m1 = jnp.max(lg, axis=-1, keepdims=True)
    i1 = jnp.min(jnp.where(lg == m1, lane, LANES), axis=-1, keepdims=True)
    lg2 = jnp.where(lane == i1, neg, lg)
    m2 = jnp.max(lg2, axis=-1, keepdims=True)
    i2 = jnp.min(jnp.where(lg2 == m2, lane, LANES), axis=-1, keepdims=True)
    e2 = jnp.exp(m2 - m1)
    den = 1.0 + e2
    idx_ref[...] = jnp.where(lane == 0, i1, jnp.where(lane == 1, i2, 0))
    wt_ref[...] = jnp.where(lane == 0, 1.0 / den, jnp.where(lane == 1, e2 / den, 0.0))


def _router(h, w_router):
    n_tok, d = h.shape
    n_experts = w_router.shape[1]
    tm = _pick(n_tok, 1024, 8)
    w_pad = jnp.zeros((d, LANES), BF16).at[:, :n_experts].set(w_router.astype(BF16))
    out_spec = pl.BlockSpec((tm, LANES), lambda i: (i, 0))
    return pl.pallas_call(
        functools.partial(_router_kernel, n_experts=n_experts),
        grid=(n_tok // tm,),
        in_specs=[pl.BlockSpec((tm, d), lambda i: (i, 0)), pl.BlockSpec((d, LANES), lambda i: (0, 0))],
        out_specs=[out_spec, out_spec],
        out_shape=[jax.ShapeDtypeStruct((n_tok, LANES), jnp.int32), jax.ShapeDtypeStruct((n_tok, LANES), F32)],
        compiler_params=_params(1),
    )(h, w_pad)


def _route_plan(top_idx, n_experts, tm, n_tiles):
    n_tok = top_idx.shape[0]
    e_flat = top_idx.reshape(-1)
    onehot = (e_flat[:, None] == jnp.arange(n_experts, dtype=jnp.int32)[None, :]).astype(jnp.int32)
    csum = jnp.cumsum(onehot, axis=0)
    rank = jnp.sum((csum - onehot) * onehot, axis=1)
    tiles_e = (csum[-1] + tm - 1) // tm
    tiles_end = jnp.cumsum(tiles_e)
    pos = (tiles_end - tiles_e)[e_flat] * tm + rank
    n_used = tiles_end[-1]
    tile_ids = jnp.arange(n_tiles, dtype=jnp.int32)
    tile_expert = jnp.sum((tile_ids[:, None] >= tiles_end[None, :]).astype(jnp.int32), axis=1)
    tile_expert = jnp.minimum(tile_expert, n_experts - 1)
    last_expert = tile_expert[jnp.maximum(n_used - 1, 0)]
    tile_expert = jnp.where(tile_ids < n_used, tile_expert, last_expert)
    row_token = jnp.zeros((n_tiles * tm,), jnp.int32).at[pos].set(jnp.arange(TOP_K * n_tok, dtype=jnp.int32) // TOP_K)
    return row_token, pos.reshape(n_tok, TOP_K).astype(jnp.int32), tile_expert, n_used.reshape(1).astype(jnp.int32)


def _gather_kernel(tok_ref, src_ref, o_ref, buf_ref, sem, *, tg):
    def issue(r, carry):
        pltpu.make_async_copy(src_ref.at[pl.ds(tok_ref[0, r], 1)], buf_ref.at[pl.ds(r, 1)], sem).start()
        return carry

    def drain(r, carry):
        pltpu.make_async_copy(src_ref.at[pl.ds(0, 1)], buf_ref.at[pl.ds(r, 1)], sem).wait()
        return carry

    lax.fori_loop(0, tg, issue, 0, unroll=8)
    lax.fori_loop(0, tg, drain, 0, unroll=8)
    o_ref[...] = buf_ref[...].astype(o_ref.dtype)


def _gather_rows(src, row_token, tg, out_dtype):
    n_rows = row_token.shape[0]
    d = src.shape[1]
    return pl.pallas_call(
        functools.partial(_gather_kernel, tg=tg),
        grid=(n_rows // tg,),
        in_specs=[pl.BlockSpec((None, 1, tg), lambda i: (i, 0, 0), memory_space=pltpu.SMEM),
                  pl.BlockSpec(memory_space=pl.ANY)],
        out_specs=pl.BlockSpec((tg, d), lambda i: (i, 0)),
        out_shape=jax.ShapeDtypeStruct((n_rows, d), out_dtype),
        scratch_shapes=[pltpu.VMEM((tg, d), src.dtype), pltpu.SemaphoreType.DMA(())],
        compiler_params=_params(1),
    )(row_token.reshape(n_rows // tg, 1, tg), src)


def _moe_ffn_kernel(te_ref, nv_ref, xs_ref, wg_ref, wu_ref, wo_ref, o_ref):
    i, f = pl.program_id(0), pl.program_id(1)
    valid = i < nv_ref[0]

    @pl.when(f == 0)
    def _():
        o_ref[...] = jnp.zeros_like(o_ref)

    @pl.when(valid)
    def _():
        xb = xs_ref[...]
        gate = jnp.dot(xb, wg_ref[...], preferred_element_type=F32)
        up = jnp.dot(xb, wu_ref[...], preferred_element_type=F32)
        o_ref[...] += jnp.dot((_silu(gate) * up).astype(BF16), wo_ref[...], preferred_element_type=F32)


def _moe_ffn(xs, w_in, w_out, tile_expert, n_used, tm):
    n_rows, d = xs.shape
    ff = w_out.shape[1]
    tf = _pick(ff, 512, LANES)
    nf = ff // tf

    def f_eff(i, f, nv):
        return jnp.where(i < nv[0], f, nf - 1)

    grid_spec = pltpu.PrefetchScalarGridSpec(
        num_scalar_prefetch=2,
        grid=(n_rows // tm, nf),
        in_specs=[pl.BlockSpec((tm, d), lambda i, f, te, nv: (jnp.minimum(i, jnp.maximum(nv[0] - 1, 0)), 0)),
                  pl.BlockSpec((None, d, tf), lambda i, f, te, nv: (te[i], 0, f_eff(i, f, nv))),
                  pl.BlockSpec((None, d, tf), lambda i, f, te, nv: (te[i], 0, nf + f_eff(i, f, nv))),
                  pl.BlockSpec((None, tf, d), lambda i, f, te, nv: (te[i], f_eff(i, f, nv), 0))],
        out_specs=pl.BlockSpec((tm, d), lambda i, f, te, nv: (i, 0)))
    return pl.pallas_call(
        _moe_ffn_kernel, grid_spec=grid_spec,
        out_shape=jax.ShapeDtypeStruct((n_rows, d), F32),
        compiler_params=_params(2),
    )(tile_expert, n_used, xs, w_in, w_in, w_out)


def _combine_kernel(pos_ref, ys_ref, wt_ref, x_ref, g_ref, ml_ref, mc_ref, o_ref, buf_ref, sem, *,
                    k_gate, tc, n_lat, has_ctx):
    def issue(r, carry):
        for k in range(TOP_K):
            pltpu.make_async_copy(ys_ref.at[pl.ds(pos_ref[0, TOP_K * r + k], 1)],
                                  buf_ref.at[k, pl.ds(r, 1)], sem).start()
        return carry

    def drain(r, carry):
        for k in range(TOP_K):
            pltpu.make_async_copy(ys_ref.at[pl.ds(0, 1)], buf_ref.at[k, pl.ds(r, 1)], sem).wait()
        return carry

    lax.fori_loop(0, tc, issue, 0)
    lax.fori_loop(0, tc, drain, 0)
    wt = wt_ref[...]
    f = wt[:, 0:1] * buf_ref[0]
    for k in range(1, TOP_K):
        f = f + wt[:, k:k + 1] * buf_ref[k]
    o_ref[...] = _gated_residual(x_ref[...], f, g_ref, ml_ref, mc_ref, k_gate, pl.program_id(1), tc, n_lat, has_ctx)


def _combine(ys, pos, wts, x, g, mods, rows, n_lat, k_gate):
    n_batch, _, d = x.shape
    tc = _pick(rows, 256, 8)
    nj = rows // tc
    kern = functools.partial(_combine_kernel, k_gate=k_gate, tc=tc, n_lat=n_lat, has_ctx=rows > n_lat)
    row_spec = pl.BlockSpec((None, tc, d), lambda b, j: (b, j, 0))
    return pl.pallas_call(
        kern,
        grid=(n_batch, nj),
        in_specs=[pl.BlockSpec((None, 1, TOP_K * tc), lambda b, j: (b * nj + j, 0, 0), memory_space=pltpu.SMEM),
                  pl.BlockSpec(memory_space=pl.ANY),
                  pl.BlockSpec((tc, LANES), lambda b, j: (b * nj + j, 0)),
                  row_spec,
                  pl.BlockSpec((1, d), lambda b, j: (0, 0)),
                  pl.BlockSpec((None, 6, 1, d), lambda b, j: (b, 0, 0, 0)),
                  pl.BlockSpec((None, 6, 1, d), lambda b, j: (n_batch, 0, 0, 0))],
        out_specs=row_spec,
        out_shape=jax.ShapeDtypeStruct((n_batch, rows, d), F32),
        scratch_shapes=[pltpu.VMEM((TOP_K, tc, d), F32), pltpu.SemaphoreType.DMA(())],
        compiler_params=_params(2),
    )(pos.reshape(n_batch * nj, 1, TOP_K * tc), ys, wts, x, g.reshape(1, d), mods, mods)


def _ffn_moe(h, w_router, w_in, w_out, x, g, mods, rows, n_lat, k_gate):
    n_batch, _, d = h.shape
    n_tok = n_batch * rows
    n_experts = w_router.shape[1]
    tm = min(768, max(16, (TOP_K * n_tok // n_experts) // 16 * 16))
    n_tiles = (TOP_K * n_tok + n_experts * (tm - 1)) // tm
    h_flat = h.reshape(n_tok, d)
    idx, wts = _router(h_flat, w_router)
    row_token, pos, tile_expert, n_used = _route_plan(idx[:, :TOP_K], n_experts, tm, n_tiles)
    xs = _gather_rows(h_flat, row_token, tm, BF16)
    ys = _moe_ffn(xs, w_in, w_out, tile_expert, n_used, tm)
    return _combine(ys, pos, wts, x, g, mods, rows, n_lat, k_gate)


def kernel(x, c, ctx, c_ctx, w_mod, b_mod, norm_g, w_in, qk_norm_g, diff_lambda, diff_subln_g, ret_log_decay,
           w_branch, w_branch_gate, w_out, ffn_w_in, ffn_w_out, moe_router, moe_w_in, moe_w_out):
    n_batch, n_lat, d = x.shape
    n_ctx = ctx.shape[1]
    depth = w_mod.shape[0]
    a_heads, b_heads, c_heads = d // 256, d // 512, d // 256
    kinds, off = _proj_layout(d)

    mod_rows = (n_batch + 1 + 7) // 8 * 8
    cond = jnp.zeros((mod_rows, d), F32).at[:n_batch].set(c).at[n_batch].set(c_ctx)
    mods_all = _modvec(cond, w_mod, b_mod).reshape(depth, mod_rows, 6, 1, d)
    tabs = _rope_tables(n_lat, n_ctx)

    xs = jnp.concatenate([x, ctx], axis=1)
    n_all = n_lat + n_ctx
    for i in range(depth):
        rows = n_all if i < depth - 1 else n_lat
        lam_init = 0.8 - 0.6 * math.exp(-0.3 * i)
        mods = mods_all[i]
        h = _norm_mod(xs, norm_g[i, 0], mods, n_all, n_lat, 0, 1, BF16)
        p = _proj(h, w_in[i].astype(BF16), tabs, qk_norm_g[i], kinds)
        o_a = _attention(p, diff_lambda[i], diff_subln_g[i], rows, n_lat, n_ctx, a_heads,
                         off["aq"], off["ak"], off["av"], 1, True, lam_init)
        o_b = _retention(p, ret_log_decay[i], rows, n_lat, n_ctx, b_heads, off)
        o_c = _attention(p, diff_lambda[i], diff_subln_g[i], rows, n_lat, n_ctx, c_heads,
                         off["cq"], off["ck"], off["cv"], C_GROUP, False, lam_init)
        y = _merge(h, (o_a, o_b, o_c), w_branch_gate[i].astype(BF16), w_branch[i].astype(BF16), rows)
        xs = _out_resid(y, w_out[i].astype(BF16), xs, norm_g[i, 1], mods, rows, n_lat, 2)
        j = i // 2
        if i % 2 == 0:
            h2 = _norm_mod(xs, norm_g[i, 2], mods, rows, n_lat, 3, 4, BF16)
            xs = _ffn_dense(h2, ffn_w_in[j].astype(BF16), ffn_w_out[j].astype(BF16), xs, norm_g[i, 3], mods,
                            rows, n_lat, 5)
        else:
            h2 = _norm_mod(xs, norm_g[i, 2], mods, rows, n_lat, 3, 4, F32)
            xs = _ffn_moe(h2, moe_router[j], moe_w_in[j].astype(BF16), moe_w_out[j].astype(BF16), xs,
                          norm_g[i, 3], mods, rows, n_lat, 5)
    return xs[:, :n_lat]
```

```python
import functools
import math

import jax
import jax.numpy as jnp
from jax import lax
from jax.experimental import pallas as pl
from jax.experimental.pallas import tpu as pltpu

F32 = jnp.float32
BF16 = jnp.bfloat16

GRID_W = 64
ROPE_THETA = 10000.0
EPS = 1e-6
GN_EPS = 1e-5
A_DIM = 64
B_QK = 128
B_V = 256
C_DIM = 128
C_GROUP = 4
TOP_K = 2
K_SHIFT_TOK, K_SCALE_TOK, K_GATE_TOK, K_SHIFT_CH, K_SCALE_CH, K_GATE_CH = range(6)
LANES = 128
SUBLANES = 8
LOG2E = 1.4426950408889634
BOUND_SLACK = 1.02
DEN_FLOOR = 2.0 ** -88
VMEM_LIMIT_BYTES = 56 * 1024 * 1024


def _pick(n, target, mult):
    best = None
    for t in range(mult, min(n, target) + 1, mult):
        if n % t == 0:
            best = t
    return best if best is not None else n


def _params(n_axes):
    return pltpu.CompilerParams(dimension_semantics=("arbitrary",) * n_axes,
                                vmem_limit_bytes=VMEM_LIMIT_BYTES)


def _silu(v):
    return v * jax.nn.sigmoid(v)


def _rms(v, eps):
    return v * lax.rsqrt(jnp.mean(v * v, axis=-1, keepdims=True) + eps)


def _ctx_rows(tile_idx, tm, n_lat, has_ctx):
    if not has_ctx:
        return None
    row = tile_idx * tm + lax.broadcasted_iota(jnp.int32, (tm, 1), 0)
    return row >= n_lat


def _mod_vec(ml_ref, mc_ref, k, is_ctx):
    return ml_ref[k] if is_ctx is None else jnp.where(is_ctx, mc_ref[k], ml_ref[k])


def _modulated_norm(x, g_ref, ml_ref, mc_ref, k_shift, k_scale, is_ctx):
    y = _rms(x, EPS) * g_ref[...]
    return y * (1.0 + _mod_vec(ml_ref, mc_ref, k_scale, is_ctx)) + _mod_vec(ml_ref, mc_ref, k_shift, is_ctx)


def _gated_residual(x, f, g_ref, ml_ref, mc_ref, k_gate, is_ctx):
    return x + _mod_vec(ml_ref, mc_ref, k_gate, is_ctx) * (_rms(f, EPS) * g_ref[...])


def _modvec_kernel(c_ref, w_ref, b_ref, o_ref):
    a = _silu(c_ref[...]).astype(BF16)
    o_ref[...] = jnp.dot(a, w_ref[...].astype(BF16), preferred_element_type=F32) + b_ref[...]


def _modvec(cond, w_mod, b_mod):
    depth, d, n6 = w_mod.shape
    rows = cond.shape[0]
    tn = _pick(n6, 1024, LANES)
    return pl.pallas_call(
        _modvec_kernel,
        grid=(depth, n6 // tn),
        in_specs=[pl.BlockSpec((rows, d), lambda i, n: (0, 0)),
                  pl.BlockSpec((None, d, tn), lambda i, n: (i, 0, n)),
                  pl.BlockSpec((None, 1, tn), lambda i, n: (i, 0, n))],
        out_specs=pl.BlockSpec((None, rows, tn), lambda i, n: (i, 0, n)),
        out_shape=jax.ShapeDtypeStruct((depth, rows, n6), F32),
        compiler_params=_params(2),
    )(cond, w_mod, b_mod.reshape(depth, 1, n6))


def _norm_mod_kernel(x_ref, g_ref, ml_ref, mc_ref, o_ref, *, k_shift, k_scale, n_lat, tm, has_ctx):
    is_ctx = _ctx_rows(pl.program_id(1), tm, n_lat, has_ctx)
    o_ref[...] = _modulated_norm(x_ref[...], g_ref, ml_ref, mc_ref, k_shift, k_scale, is_ctx).astype(o_ref.dtype)


def _norm_mod(x, g, mods, rows, n_lat, k_shift, k_scale, out_dtype):
    n_batch, _, d = x.shape
    tm = _pick(rows, 512, 16)
    kern = functools.partial(_norm_mod_kernel, k_shift=k_shift, k_scale=k_scale, n_lat=n_lat, tm=tm,
                             has_ctx=rows > n_lat)
    return pl.pallas_call(
        kern,
        grid=(n_batch, rows // tm),
        in_specs=[pl.BlockSpec((None, tm, d), lambda b, j: (b, j, 0)),
                  pl.BlockSpec((1, d), lambda b, j: (0, 0)),
                  pl.BlockSpec((None, 6, 1, d), lambda b, j: (b, 0, 0, 0)),
                  pl.BlockSpec((None, 6, 1, d), lambda b, j: (n_batch, 0, 0, 0))],
        out_specs=pl.BlockSpec((None, tm, d), lambda b, j: (b, j, 0)),
        out_shape=jax.ShapeDtypeStruct((n_batch, rows, d), out_dtype),
        compiler_params=_params(2),
    )(x, g.reshape(1, d), mods, mods)


def _rope_tables(n_lat, n_ctx):
    rows = n_lat // GRID_W
    row = jnp.repeat(jnp.arange(rows, dtype=F32), GRID_W)
    col = jnp.tile(jnp.arange(GRID_W, dtype=F32), rows)

    def angles(dim):
        n = dim // 4
        inv = ROPE_THETA ** (-jnp.arange(n, dtype=F32) / n)
        return jnp.concatenate([row[:, None] * inv, col[:, None] * inv], axis=-1)

    a64, a128 = angles(A_DIM), angles(C_DIM)
    lane = jnp.arange(LANES)
    c64, s64 = jnp.cos(a64)[:, lane % 32], jnp.sin(a64)[:, lane % 32]
    first_half = (lane % 64) < 32
    sa64 = jnp.where(first_half, -s64, 0.0)
    sb64 = jnp.where(first_half, 0.0, s64)
    c128 = jnp.cos(a128)[:, lane % 64]
    s128 = jnp.sin(a128)[:, lane % 64] * jnp.where(lane < 64, -1.0, 1.0)
    lat = jnp.stack([c64, sa64, sb64, c128, s128])
    ident = jnp.zeros((5, n_ctx, LANES), F32).at[jnp.array([0, 3])].set(1.0)
    return jnp.concatenate([lat, ident], axis=1)


def _proj_epilogue(kind, x, tab_ref, qkg_ref):
    name = kind[0]
    if name == "none":
        return x
    if name == "silu":
        return _silu(x)
    if name == "rope64":
        y = (x * tab_ref[0] + pltpu.roll(x, 96, 1) * tab_ref[1] + pltpu.roll(x, 32, 1) * tab_ref[2])
        return y * kind[1]
    if name == "norm_rope128":
        x = _rms(x, EPS) * qkg_ref[kind[2]:kind[2] + 1, :]
    y = x * tab_ref[3] + pltpu.roll(x, 64, 1) * tab_ref[4]
    return y * kind[1]


def _proj_kernel(h_ref, w_ref, tab_ref, qkg_ref, o_ref, acc_ref, *, groups, n_tiles):
    n = pl.program_id(2)
    slot = lax.rem(n, 2)

    def matmul():
        acc_ref[slot] = jnp.dot(h_ref[...], w_ref[...], preferred_element_type=F32)

    def epilogue(kinds):
        prev = acc_ref.at[1 - slot]
        for c, kind in enumerate(kinds):
            sl = slice(c * LANES, (c + 1) * LANES)
            o_ref[:, sl] = _proj_epilogue(kind, prev[:, sl], tab_ref, qkg_ref).astype(o_ref.dtype)

    pl.when(n == 0)(matmul)
    for tiles, kinds in groups:
        prev_in_group = functools.reduce(jnp.logical_or, [n - 1 == t for t in tiles])

        @pl.when(prev_in_group & (n < n_tiles))
        def _(kinds=kinds):
            epilogue(kinds)
            matmul()

        if n_tiles - 1 in tiles:
            pl.when(n == n_tiles)(functools.partial(epilogue, kinds))


def _proj_layout(d):
    a_heads, b_heads, c_heads = d // 256, d // 512, d // 256
    c_kv = c_heads // C_GROUP
    segs = [("aq", a_heads, ("rope64", A_DIM ** -0.5 * LOG2E)), ("ak", a_heads, ("rope64", 1.0)),
            ("av", a_heads, ("none",)),
            ("bq", b_heads, ("rope128", 1.0)), ("bk", b_heads, ("rope128", B_QK ** -0.5)),
            ("bv", 2 * b_heads, ("none",)), ("bg", 2 * b_heads, ("silu",)),
            ("cq", c_heads, ("norm_rope128", C_DIM ** -0.5 * LOG2E, 0)), ("ck", c_kv, ("norm_rope128", 1.0, 1)),
            ("cv", c_kv, ("none",))]
    kinds, off = [], {}
    for name, nblk, kind in segs:
        off[name] = len(kinds)
        kinds += [kind] * nblk
    return kinds, off


def _proj(h, w, tabs, qkg, kinds):
    n_batch, rows, d = h.shape
    width = w.shape[1]
    tm = _pick(rows, 1152, 16)
    tn = _pick(width, 512, LANES)
    per = tn // LANES
    by_kinds = {}
    for t in range(width // tn):
        by_kinds.setdefault(tuple(kinds[t * per:(t + 1) * per]), []).append(t)
    groups = tuple((tuple(tiles), ks) for ks, tiles in by_kinds.items())
    n_tiles = width // tn
    return pl.pallas_call(
        functools.partial(_proj_kernel, groups=groups, n_tiles=n_tiles),
        grid=(n_batch, rows // tm, n_tiles + 1),
        in_specs=[pl.BlockSpec((None, tm, d), lambda b, j, n: (b, j, 0)),
                  pl.BlockSpec((d, tn), lambda b, j, n: (0, jnp.minimum(n, n_tiles - 1))),
                  pl.BlockSpec((5, tm, LANES), lambda b, j, n: (0, j, 0)),
                  pl.BlockSpec((2, LANES), lambda b, j, n: (0, 0))],
        out_specs=pl.BlockSpec((None, tm, tn), lambda b, j, n: (b, j, jnp.maximum(n - 1, 0))),
        out_shape=jax.ShapeDtypeStruct((n_batch, rows, width), BF16),
        scratch_shapes=[pltpu.VMEM((2, tm, tn), F32)],
        compiler_params=_params(3),
    )(h, w, tabs, qkg)


def _scores(q, k):
    return lax.dot_general(q, k, (((1,), (1,)), ((), ())), preferred_element_type=F32)


def _softmax_pv_rowmax(q, k, vx):
    s = _scores(q, k)
    e = jnp.exp2(s - jnp.max(s, axis=-1, keepdims=True)).astype(BF16)
    r = jnp.dot(e, vx, preferred_element_type=F32)
    return r[:, :LANES] / r[:, LANES:LANES + 1]


def _bound_shift(q, ksq_max):
    qf = q.astype(F32)
    return BOUND_SLACK * jnp.sqrt(jnp.sum(qf * qf, axis=-1, keepdims=True) * ksq_max)


def _attn_kernel(q_ref, k_ref, v_ref, lam_ref, g_ref, o_ref, vx_ref, ksq_ref, *,
                 diff, lam_init, n_lat_tiles, n_lat, has_ctx, heads, kv_shared):
    i = pl.program_id(2)
    n_kv = 1 if kv_shared else heads
    n_keys = vx_ref.shape[1]

    def lanes(j):
        return slice(j * LANES, (j + 1) * LANES)

    @pl.when(i == 0)
    def _():
        for j in range(n_kv):
            vx_ref[j, :, :LANES] = v_ref[:, lanes(j)]
            vx_ref[j, :, LANES:] = jnp.ones((n_keys, LANES), BF16)
            k = k_ref[:, lanes(j)]
            ksq = jnp.dot(k * k, jnp.ones((LANES, LANES), BF16), preferred_element_type=F32)
            ksq_ref[j] = jnp.broadcast_to(jnp.max(ksq, axis=0, keepdims=True), (SUBLANES, LANES))

    def run(key_rows, nq):
        def operands(hh):
            j = 0 if kv_shared else hh
            q = q_ref[:nq, lanes(hh)]
            if diff:
                qf = q.astype(F32)
                lane = lax.broadcasted_iota(jnp.int32, qf.shape, 1)
                qs = (jnp.where(lane < A_DIM, qf, 0.0).astype(BF16), jnp.where(lane >= A_DIM, qf, 0.0).astype(BF16))
            else:
                qs = (q,)
            return qs, k_ref[key_rows, lanes(j)], vx_ref[j, key_rows, :], ksq_ref[j, 0:1, 0:1]

        def finish(hh, outs):
            if diff:
                lv = lam_ref[...]
                lam = (jnp.exp(jnp.sum(lv[0:1] * lv[1:2], axis=-1, keepdims=True))
                       - jnp.exp(jnp.sum(lv[2:3] * lv[3:4], axis=-1, keepdims=True)) + lam_init)
                o = _rms(outs[0] - lam * outs[1], EPS) * g_ref[...] * (1.0 - lam_init)
            else:
                o = outs[0]
            o_ref[:nq, lanes(hh)] = o.astype(o_ref.dtype)

        ops = [(hh, qq, k, vx, ksq) for hh in range(heads) for qs, k, vx, ksq in [operands(hh)] for qq in qs]
        shifted = [_scores(qq, k) - _bound_shift(qq, ksq) for _, qq, k, _, ksq in ops]
        exps = [jnp.exp2(s).astype(BF16) for s in shifted]
        prods = [jnp.dot(e, op[3], preferred_element_type=F32) for e, op in zip(exps, ops)]
        parts = [[] for _ in range(heads)]
        for op, r in zip(ops, prods):
            parts[op[0]].append((r[:, :LANES], r[:, LANES:LANES + 1]))
        den_min = functools.reduce(jnp.minimum, [den for head in parts for _, den in head])
        precise = jnp.min(den_min) >= DEN_FLOOR

        @pl.when(precise)
        def _():
            for hh in range(heads):
                finish(hh, [num / den for num, den in parts[hh]])

        @pl.when(jnp.logical_not(precise))
        def _():
            for hh in range(heads):
                qs, k, vx, _ = operands(hh)
                finish(hh, [_softmax_pv_rowmax(qq, k, vx) for qq in qs])

    tq = q_ref.shape[0]
    if not has_ctx:
        run(slice(None), tq)
        return

    @pl.when(i < n_lat_tiles)
    def _():
        run(slice(None), tq)

    @pl.when(i >= n_lat_tiles)
    def _():
        run(slice(n_lat, None), min(tq, n_keys - n_lat))


def _query_tile(n_lat, n_ctx, has_ctx):
    tq = _pick(n_lat, 512, 16)
    assert not has_ctx or n_ctx % tq == 0 or n_ctx < tq
    return tq


def _attention(p, lam_vec, subln_g, rows, n_lat, n_ctx, n_heads, q_off, k_off, v_off, kv_group, diff, lam_init):
    n_batch, n_keys, _ = p.shape
    has_ctx = rows > n_lat
    tq = _query_tile(n_lat, n_ctx, has_ctx)
    kv_shared = kv_group > 1
    heads = kv_group if kv_shared else 2
    assert n_heads % heads == 0 and q_off % heads == 0 and (kv_shared or (k_off % heads == 0 and v_off % heads == 0))
    kv_w = LANES if kv_shared else heads * LANES
    kv_blk = (lambda off: (lambda b, h, i: (b, 0, off + h))) if kv_shared else \
             (lambda off: (lambda b, h, i: (b, 0, off // heads + h)))
    kern = functools.partial(_attn_kernel, diff=diff, lam_init=lam_init, n_lat_tiles=n_lat // tq,
                             n_lat=n_lat, has_ctx=has_ctx, heads=heads, kv_shared=kv_shared)
    n_kv = 1 if kv_shared else heads
    return pl.pallas_call(
        kern,
        grid=(n_batch, n_heads // heads, pl.cdiv(rows, tq)),
        in_specs=[pl.BlockSpec((None, tq, heads * LANES), lambda b, h, i: (b, i, q_off // heads + h)),
                  pl.BlockSpec((None, n_keys, kv_w), kv_blk(k_off)),
                  pl.BlockSpec((None, n_keys, kv_w), kv_blk(v_off)),
                  pl.BlockSpec(lam_vec.shape, lambda b, h, i: (0, 0)),
                  pl.BlockSpec((1, LANES), lambda b, h, i: (0, 0))],
        out_specs=pl.BlockSpec((None, tq, heads * LANES), lambda b, h, i: (b, i, h)),
        out_shape=jax.ShapeDtypeStruct((n_batch, rows, n_heads * LANES), BF16),
        scratch_shapes=[pltpu.VMEM((n_kv, n_keys, 2 * LANES), BF16), pltpu.VMEM((n_kv, SUBLANES, LANES), F32)],
        compiler_params=_params(3),
    )(p, p, p, lam_vec, subln_g.reshape(1, LANES))


def _ret_decay(lgf, lgb, t, pf, pb):
    df = (t - pf).astype(F32)
    db = (pb - t).astype(F32)
    fwd = jnp.where(df >= 0, jnp.exp(lgf * jnp.maximum(df, 0.0)), 0.0)
    bwd = jnp.where(db >= 0, jnp.exp(lgb * jnp.maximum(db, 0.0)), 0.0)
    return fwd + bwd


def _ret_kernel(ld_ref, q_ref, k_ref, v_ref, g_ref, o_ref, d_ref, *, tq, n_keys, n_lat, n_heads, has_ctx):
    h, i, b = pl.program_id(0), pl.program_id(1), pl.program_id(2)
    n_ctx = n_keys - n_lat
    n_lat_tiles = n_lat // tq
    lgf, lgb = ld_ref[h], ld_ref[n_heads + h]
    t_loc = lax.broadcasted_iota(jnp.int32, (tq, 1), 0)

    def lat_decay():
        col = lax.broadcasted_iota(jnp.int32, (1, n_keys), 1)
        pf = jnp.where(col < n_lat, col, col - n_keys)
        d_ref[...] = _ret_decay(lgf, lgb, i * tq + t_loc, pf, col)

    def ctx_decay():
        col = lax.broadcasted_iota(jnp.int32, (1, n_ctx), 1)
        d_ref[:, :n_ctx] = _ret_decay(lgf, lgb, (i - n_lat_tiles) * tq + t_loc, col, col)

    def run(k, v, decay, nq):
        s = lax.dot_general(q_ref[:nq, :], k, (((1,), (1,)), ((), ())), preferred_element_type=F32)
        o = jnp.dot((s * decay).astype(BF16), v, preferred_element_type=F32)
        mu = jnp.mean(o, axis=-1, keepdims=True)
        var = jnp.mean(jnp.square(o - mu), axis=-1, keepdims=True)
        o_ref[:nq, :] = ((o - mu) * lax.rsqrt(var + GN_EPS) * g_ref[:nq, :].astype(F32)).astype(o_ref.dtype)

    if not has_ctx:
        pl.when(b == 0)(lat_decay)
        run(k_ref[...], v_ref[...], d_ref[...], tq)
        return

    pl.when((b == 0) & (i < n_lat_tiles))(lat_decay)
    pl.when((b == 0) & (i >= n_lat_tiles))(ctx_decay)

    @pl.when(i < n_lat_tiles)
    def _():
        run(k_ref[...], v_ref[...], d_ref[...], tq)

    @pl.when(i >= n_lat_tiles)
    def _():
        cq = min(tq, n_ctx)
        run(k_ref[n_lat:, :], v_ref[n_lat:, :], d_ref[:cq, :n_ctx], cq)


def _retention(p, log_decay, rows, n_lat, n_ctx, n_heads, off):
    n_batch, n_keys, _ = p.shape
    has_ctx = rows > n_lat
    tq = _query_tile(n_lat, n_ctx, has_ctx)
    kern = functools.partial(_ret_kernel, tq=tq, n_keys=n_keys, n_lat=n_lat, n_heads=n_heads, has_ctx=has_ctx)
    v_off, g_off = off["bv"] // 2, off["bg"] // 2
    grid_spec = pltpu.PrefetchScalarGridSpec(
        num_scalar_prefetch=1,
        grid=(n_heads, pl.cdiv(rows, tq), n_batch),
        in_specs=[pl.BlockSpec((None, tq, B_QK), lambda h, i, b, ld: (b, i, off["bq"] + h)),
                  pl.BlockSpec((None, n_keys, B_QK), lambda h, i, b, ld: (b, 0, off["bk"] + h)),
                  pl.BlockSpec((None, n_keys, B_V), lambda h, i, b, ld: (b, 0, v_off + h)),
                  pl.BlockSpec((None, tq, B_V), lambda h, i, b, ld: (b, i, g_off + h))],
        out_specs=pl.BlockSpec((None, tq, B_V), lambda h, i, b, ld: (b, i, h)),
        scratch_shapes=[pltpu.VMEM((tq, n_keys), F32)])
    return pl.pallas_call(
        kern, grid_spec=grid_spec,
        out_shape=jax.ShapeDtypeStruct((n_batch, rows, n_heads * B_V), BF16),
        compiler_params=_params(3),
    )(log_decay.reshape(-1).astype(F32), p, p, p, p)


def _merge_kernel(h_ref, oa_ref, ob_ref, oc_ref, ga_ref, gb_ref, gc_ref, wb_ref, y_ref):
    h = h_ref[...]
    y = None
    for idx, (o_ref, wg_ref) in enumerate(((oa_ref, ga_ref), (ob_ref, gb_ref), (oc_ref, gc_ref))):
        gate = jax.nn.sigmoid(jnp.dot(h, wg_ref[...], preferred_element_type=F32))
        term = gate * jnp.dot(o_ref[...], wb_ref[idx], preferred_element_type=F32)
        y = term if y is None else y + term
    y_ref[...] = y.astype(y_ref.dtype)


def _merge(h, outs, w_gate, w_branch, rows):
    n_batch, _, d = h.shape
    bw = w_branch.shape[1]
    tm = _pick(rows, 1152, 16)
    tn = _pick(d, 512, LANES)
    nn = d // tn
    o_spec = pl.BlockSpec((None, tm, bw), lambda b, j, n: (b, j, 0))
    g_specs = [pl.BlockSpec((d, tn), functools.partial(lambda b, j, n, k: (0, k * nn + n), k=k)) for k in range(3)]
    return pl.pallas_call(
        _merge_kernel,
        grid=(n_batch, rows // tm, nn),
        in_specs=[pl.BlockSpec((None, tm, d), lambda b, j, n: (b, j, 0)), o_spec, o_spec, o_spec,
                  *g_specs, pl.BlockSpec((3, bw, tn), lambda b, j, n: (0, 0, n))],
        out_specs=pl.BlockSpec((None, tm, tn), lambda b, j, n: (b, j, n)),
        out_shape=jax.ShapeDtypeStruct((n_batch, rows, d), BF16),
        compiler_params=_params(3),
    )(h, *outs, w_gate, w_gate, w_gate, w_branch)


def _out_resid_kernel(y_ref, w_ref, x_ref, g_ref, gn_ref, ml_ref, mc_ref, o_ref, hn_ref, *, tm, n_lat, has_ctx):
    is_ctx = _ctx_rows(pl.program_id(1), tm, n_lat, has_ctx)
    f = jnp.dot(y_ref[...], w_ref[...], preferred_element_type=F32)
    x_new = _gated_residual(x_ref[...], f, g_ref, ml_ref, mc_ref, K_GATE_TOK, is_ctx)
    o_ref[...] = x_new
    hn_ref[...] = _modulated_norm(x_new, gn_ref, ml_ref, mc_ref, K_SHIFT_CH, K_SCALE_CH, is_ctx).astype(hn_ref.dtype)


def _out_resid(y, w_out, x, g, g_next, mods, rows, n_lat, next_dtype):
    n_batch, _, d = y.shape
    tm = _pick(rows, 512, 16)
    kern = functools.partial(_out_resid_kernel, tm=tm, n_lat=n_lat, has_ctx=rows > n_lat)
    row_spec = pl.BlockSpec((None, tm, d), lambda b, j: (b, j, 0))
    vec_spec = pl.BlockSpec((1, d), lambda b, j: (0, 0))
    return pl.pallas_call(
        kern,
        grid=(n_batch, rows // tm),
        in_specs=[row_spec, pl.BlockSpec((d, d), lambda b, j: (0, 0)), row_spec, vec_spec, vec_spec,
                  pl.BlockSpec((None, 6, 1, d), lambda b, j: (b, 0, 0, 0)),
                  pl.BlockSpec((None, 6, 1, d), lambda b, j: (n_batch, 0, 0, 0))],
        out_specs=[row_spec, row_spec],
        out_shape=[jax.ShapeDtypeStruct((n_batch, rows, d), F32), jax.ShapeDtypeStruct((n_batch, rows, d), next_dtype)],
        compiler_params=_params(2),
    )(y, w_out, x, g.reshape(1, d), g_next.reshape(1, d), mods, mods)


def _ffn_kernel(*refs, tm, n_lat, has_ctx, has_next):
    if has_next:
        (h_ref, wg_ref, wu_ref, wo_ref, x_ref, g_ref, ml_ref, mc_ref, gn_ref, mln_ref, mcn_ref,
         o_ref, hn_ref) = refs
    else:
        h_ref, wg_ref, wu_ref, wo_ref, x_ref, g_ref, ml_ref, mc_ref, o_ref = refs
    f = pl.program_id(2)
    h = h_ref[...]
    gate = jnp.dot(h, wg_ref[...], preferred_element_type=F32)
    up = jnp.dot(h, wu_ref[...], preferred_element_type=F32)
    part = jnp.dot((_silu(gate) * up).astype(BF16), wo_ref[...], preferred_element_type=F32)

    @pl.when(f == 0)
    def _():
        o_ref[...] = part

    @pl.when(f > 0)
    def _():
        o_ref[...] += part

    @pl.when(f == pl.num_programs(2) - 1)
    def _():
        is_ctx = _ctx_rows(pl.program_id(1), tm, n_lat, has_ctx)
        x_new = _gated_residual(x_ref[...], o_ref[...], g_ref, ml_ref, mc_ref, K_GATE_CH, is_ctx)
        o_ref[...] = x_new
        if has_next:
            hn_ref[...] = _modulated_norm(x_new, gn_ref, mln_ref, mcn_ref, K_SHIFT_TOK, K_SCALE_TOK,
                                          is_ctx).astype(hn_ref.dtype)


def _ffn_dense(h, w_in, w_out, x, g, mods, rows, n_lat, g_next, mods_next):
    n_batch, _, d = h.shape
    ff = w_out.shape[0]
    tm = _pick(rows, 576, 16)
    tf = _pick(ff, 512, LANES)
    nf = ff // tf
    has_next = g_next is not None
    kern = functools.partial(_ffn_kernel, tm=tm, n_lat=n_lat, has_ctx=rows > n_lat, has_next=has_next)
    row_spec = pl.BlockSpec((None, tm, d), lambda b, j, f: (b, j, 0))
    vec_spec = pl.BlockSpec((1, d), lambda b, j, f: (0, 0))
    mod_specs = [pl.BlockSpec((None, 6, 1, d), lambda b, j, f: (b, 0, 0, 0)),
                 pl.BlockSpec((None, 6, 1, d), lambda b, j, f: (n_batch, 0, 0, 0))]
    in_specs = [row_spec,
                pl.BlockSpec((d, tf), lambda b, j, f: (0, f)),
                pl.BlockSpec((d, tf), lambda b, j, f: (0, nf + f)),
                pl.BlockSpec((tf, d), lambda b, j, f: (f, 0)),
                row_spec, vec_spec, *mod_specs]
    args = [h, w_in, w_in, w_out, x, g.reshape(1, d), mods, mods]
    out_specs, out_shape = [row_spec], [jax.ShapeDtypeStruct((n_batch, rows, d), F32)]
    if has_next:
        in_specs += [vec_spec, *mod_specs]
        args += [g_next.reshape(1, d), mods_next, mods_next]
        out_specs.append(row_spec)
        out_shape.append(jax.ShapeDtypeStruct((n_batch, rows, d), BF16))
    outs = pl.pallas_call(
        kern,
        grid=(n_batch, rows // tm, nf),
        in_specs=in_specs, out_specs=out_specs, out_shape=out_shape,
        compiler_params=_params(3),
    )(*args)
    return (outs[0], outs[1]) if has_next else (outs[0], None)


def _router_kernel(h_ref, w_ref, idx_ref, wt_ref, *, n_experts):
    logits = jnp.dot(h_ref[...].astype(BF16), w_ref[...], preferred_element_type=F32)
    lane = lax.broadcasted_iota(jnp.int32, logits.shape, 1)
    neg = jnp.float32(-jnp.inf)
    lg = jnp.where(lane < n_experts, logits, neg)
    m1 = jnp.max(lg, axis=-1, keepdims=True)
    i1 = jnp.min(jnp.where(lg == m1, lane, LANES), axis=-1, keepdims=True)
    lg2 = jnp.where(lane == i1, neg, lg)
    m2 = jnp.max(lg2, axis=-1, keepdims=True)
    i2 = jnp.min(jnp.where(lg2 == m2, lane, LANES), axis=-1, keepdims=True)
    e2 = jnp.exp(m2 - m1)
    den = 1.0 + e2
    idx_ref[...] = jnp.where(lane == 0, i1, jnp.where(lane == 1, i2, 0))
    wt_ref[...] = jnp.where(lane == 0, 1.0 / den, jnp.where(lane == 1, e2 / den, 0.0))


def _router(h, w_router):
    n_tok, d = h.shape
    n_experts = w_router.shape[1]
    tm = _pick(n_tok, 1024, 8)
    w_pad = jnp.zeros((d, LANES), BF16).at[:, :n_experts].set(w_router.astype(BF16))
    out_spec = pl.BlockSpec((tm, LANES), lambda i: (i, 0))
    return pl.pallas_call(
        functools.partial(_router_kernel, n_experts=n_experts),
        grid=(n_tok // tm,),
        in_specs=[pl.BlockSpec((tm, d), lambda i: (i, 0)), pl.BlockSpec((d, LANES), lambda i: (0, 0))],
        out_specs=[out_spec, out_spec],
        out_shape=[jax.ShapeDtypeStruct((n_tok, LANES), jnp.int32), jax.ShapeDtypeStruct((n_tok, LANES), F32)],
        compiler_params=_params(1),
    )(h, w_pad)


def _route_plan(top_idx, n_experts, tm, n_tiles):
    n_tok = top_idx.shape[0]
    e_flat = top_idx.reshape(-1)
    onehot = (e_flat[:, None] == jnp.arange(n_experts, dtype=jnp.int32)[None, :]).astype(jnp.int32)
    csum = jnp.cumsum(onehot, axis=0)
    rank = jnp.sum((csum - onehot) * onehot, axis=1)
    tiles_e = (csum[-1] + tm - 1) // tm
    tiles_end = jnp.cumsum(tiles_e)
    pos = (tiles_end - tiles_e)[e_flat] * tm + rank
    n_used = tiles_end[-1]
    tile_ids = jnp.arange(n_tiles, dtype=jnp.int32)
    tile_expert = jnp.sum((tile_ids[:, None] >= tiles_end[None, :]).astype(jnp.int32), axis=1)
    tile_expert = jnp.minimum(tile_expert, n_experts - 1)
    last_expert = tile_expert[jnp.maximum(n_used - 1, 0)]
    tile_expert = jnp.where(tile_ids < n_used, tile_expert, last_expert)
    row_token = jnp.zeros((n_tiles * tm,), jnp.int32).at[pos].set(jnp.arange(TOP_K * n_tok, dtype=jnp.int32) // TOP_K)
    return row_token, pos.reshape(n_tok, TOP_K).astype(jnp.int32), tile_expert, n_used.reshape(1).astype(jnp.int32)


def _gather_kernel(tok_ref, src_ref, o_ref, buf_ref, sem, *, tg):
    def issue(r, carry):
        pltpu.make_async_copy(src_ref.at[pl.ds(tok_ref[0, r], 1)], buf_ref.at[pl.ds(r, 1)], sem).start()
        return carry

    def drain(r, carry):
        pltpu.make_async_copy(src_ref.at[pl.ds(0, 1)], buf_ref.at[pl.ds(r, 1)], sem).wait()
        return carry

    lax.fori_loop(0, tg, issue, 0, unroll=8)
    lax.fori_loop(0, tg, drain, 0, unroll=8)
    o_ref[...] = buf_ref[...].astype(o_ref.dtype)


def _gather_rows(src, row_token, tg, out_dtype):
    n_rows = row_token.shape[0]
    d = src.shape[1]
    return pl.pallas_call(
        functools.partial(_gather_kernel, tg=tg),
        grid=(n_rows // tg,),
        in_specs=[pl.BlockSpec((None, 1, tg), lambda i: (i, 0, 0), memory_space=pltpu.SMEM),
                  pl.BlockSpec(memory_space=pl.ANY)],
        out_specs=pl.BlockSpec((tg, d), lambda i: (i, 0)),
        out_shape=jax.ShapeDtypeStruct((n_rows, d), out_dtype),
        scratch_shapes=[pltpu.VMEM((tg, d), src.dtype), pltpu.SemaphoreType.DMA(())],
        compiler_params=_params(1),
    )(row_token.reshape(n_rows // tg, 1, tg), src)


def _moe_ffn_kernel(te_ref, nv_ref, xs_ref, wg_ref, wu_ref, wo_ref, o_ref):
    i, f = pl.program_id(0), pl.program_id(1)
    valid = i < nv_ref[0]

    @pl.when(f == 0)
    def _():
        o_ref[...] = jnp.zeros_like(o_ref)

    @pl.when(valid)
    def _():
        xb = xs_ref[...]
        gate = jnp.dot(xb, wg_ref[...], preferred_element_type=F32)
        up = jnp.dot(xb, wu_ref[...], preferred_element_type=F32)
        o_ref[...] += jnp.dot((_silu(gate) * up).astype(BF16), wo_ref[...], preferred_element_type=F32)


def _moe_ffn(xs, w_in, w_out, tile_expert, n_used, tm):
    n_rows, d = xs.shape
    ff = w_out.shape[1]
    tf = _pick(ff, 512, LANES)
    nf = ff // tf

    def f_eff(i, f, nv):
        return jnp.where(i < nv[0], f, nf - 1)

    grid_spec = pltpu.PrefetchScalarGridSpec(
        num_scalar_prefetch=2,
        grid=(n_rows // tm, nf),
        in_specs=[pl.BlockSpec((tm, d), lambda i, f, te, nv: (jnp.minimum(i, jnp.maximum(nv[0] - 1, 0)), 0)),
                  pl.BlockSpec((None, d, tf), lambda i, f, te, nv: (te[i], 0, f_eff(i, f, nv))),
                  pl.BlockSpec((None, d, tf), lambda i, f, te, nv: (te[i], 0, nf + f_eff(i, f, nv))),
                  pl.BlockSpec((None, tf, d), lambda i, f, te, nv: (te[i], f_eff(i, f, nv), 0))],
        out_specs=pl.BlockSpec((tm, d), lambda i, f, te, nv: (i, 0)))
    return pl.pallas_call(
        _moe_ffn_kernel, grid_spec=grid_spec,
        out_shape=jax.ShapeDtypeStruct((n_rows, d), F32),
        compiler_params=_params(2),
    )(tile_expert, n_used, xs, w_in, w_in, w_out)


def _combine_kernel(pos_ref, ys_ref, wt_ref, x_ref, g_ref, ml_ref, mc_ref, o_ref, buf_ref, sem, *,
                    tc, n_lat, has_ctx):
    def issue(r, carry):
        for k in range(TOP_K):
            pltpu.make_async_copy(ys_ref.at[pl.ds(pos_ref[0, TOP_K * r + k], 1)],
                                  buf_ref.at[k, pl.ds(r, 1)], sem).start()
        return carry

    def drain(r, carry):
        for k in range(TOP_K):
            pltpu.make_async_copy(ys_ref.at[pl.ds(0, 1)], buf_ref.at[k, pl.ds(r, 1)], sem).wait()
        return carry

    lax.fori_loop(0, tc, issue, 0)
    lax.fori_loop(0, tc, drain, 0)
    wt = wt_ref[...]
    f = wt[:, 0:1] * buf_ref[0]
    for k in range(1, TOP_K):
        f = f + wt[:, k:k + 1] * buf_ref[k]
    is_ctx = _ctx_rows(pl.program_id(1), tc, n_lat, has_ctx)
    o_ref[...] = _gated_residual(x_ref[...], f, g_ref, ml_ref, mc_ref, K_GATE_CH, is_ctx)


def _combine(ys, pos, wts, x, g, mods, rows, n_lat):
    n_batch, _, d = x.shape
    tc = _pick(rows, 256, 8)
    nj = rows // tc
    kern = functools.partial(_combine_kernel, tc=tc, n_lat=n_lat, has_ctx=rows > n_lat)
    row_spec = pl.BlockSpec((None, tc, d), lambda b, j: (b, j, 0))
    return pl.pallas_call(
        kern,
        grid=(n_batch, nj),
        in_specs=[pl.BlockSpec((None, 1, TOP_K * tc), lambda b, j: (b * nj + j, 0, 0), memory_space=pltpu.SMEM),
                  pl.BlockSpec(memory_space=pl.ANY),
                  pl.BlockSpec((tc, LANES), lambda b, j: (b * nj + j, 0)),
                  row_spec,
                  pl.BlockSpec((1, d), lambda b, j: (0, 0)),
                  pl.BlockSpec((None, 6, 1, d), lambda b, j: (b, 0, 0, 0)),
                  pl.BlockSpec((None, 6, 1, d), lambda b, j: (n_batch, 0, 0, 0))],
        out_specs=row_spec,
        out_shape=jax.ShapeDtypeStruct((n_batch, rows, d), F32),
        scratch_shapes=[pltpu.VMEM((TOP_K, tc, d), F32), pltpu.SemaphoreType.DMA(())],
        compiler_params=_params(2),
    )(pos.reshape(n_batch * nj, 1, TOP_K * tc), ys, wts, x, g.reshape(1, d), mods, mods)


def _ffn_moe(h, w_router, w_in, w_out, x, g, mods, rows, n_lat):
    n_batch, _, d = h.shape
    n_tok = n_batch * rows
    n_experts = w_router.shape[1]
    tm = min(768, max(16, (TOP_K * n_tok // n_experts) // 16 * 16))
    n_tiles = (TOP_K * n_tok + n_experts * (tm - 1)) // tm
    h_flat = h.reshape(n_tok, d)
    idx, wts = _router(h_flat, w_router)
    row_token, pos, tile_expert, n_used = _route_plan(idx[:, :TOP_K], n_experts, tm, n_tiles)
    xs = _gather_rows(h_flat, row_token, tm, BF16)
    ys = _moe_ffn(xs, w_in, w_out, tile_expert, n_used, tm)
    return _combine(ys, pos, wts, x, g, mods, rows, n_lat)


def kernel(x, c, ctx, c_ctx, w_mod, b_mod, norm_g, w_in, qk_norm_g, diff_lambda, diff_subln_g, ret_log_decay,
           w_branch, w_branch_gate, w_out, ffn_w_in, ffn_w_out, moe_router, moe_w_in, moe_w_out):
    n_batch, n_lat, d = x.shape
    n_ctx = ctx.shape[1]
    depth = w_mod.shape[0]
    a_heads, b_heads, c_heads = d // 256, d // 512, d // 256
    kinds, off = _proj_layout(d)

    mod_rows = (n_batch + 1 + 7) // 8 * 8
    cond = jnp.zeros((mod_rows, d), F32).at[:n_batch].set(c).at[n_batch].set(c_ctx)
    mods_all = _modvec(cond, w_mod, b_mod).reshape(depth, mod_rows, 6, 1, d)
    tabs = _rope_tables(n_lat, n_ctx)

    xs = jnp.concatenate([x, ctx], axis=1)
    n_all = n_lat + n_ctx
    h = _norm_mod(xs, norm_g[0, 0], mods_all[0], n_all, n_lat, K_SHIFT_TOK, K_SCALE_TOK, BF16)
    for i in range(depth):
        last = i == depth - 1
        rows = n_lat if last else n_all
        lam_init = 0.8 - 0.6 * math.exp(-0.3 * i)
        mods = mods_all[i]
        dense = i % 2 == 0
        j = i // 2
        p = _proj(h, w_in[i].astype(BF16), tabs, qk_norm_g[i], kinds)
        o_a = _attention(p, diff_lambda[i], diff_subln_g[i], rows, n_lat, n_ctx, a_heads,
                         off["aq"], off["ak"], off["av"], 1, True, lam_init)
        o_b = _retention(p, ret_log_decay[i], rows, n_lat, n_ctx, b_heads, off)
        o_c = _attention(p, diff_lambda[i], diff_subln_g[i], rows, n_lat, n_ctx, c_heads,
                         off["cq"], off["ck"], off["cv"], C_GROUP, False, lam_init)
        y = _merge(h, (o_a, o_b, o_c), w_branch_gate[i].astype(BF16), w_branch[i].astype(BF16), rows)
        xs, h2 = _out_resid(y, w_out[i].astype(BF16), xs, norm_g[i, 1], norm_g[i, 2], mods, rows, n_lat,
                            BF16 if dense else F32)
        if dense:
            g_next, mods_next = (None, None) if last else (norm_g[i + 1, 0], mods_all[i + 1])
            xs, h = _ffn_dense(h2, ffn_w_in[j].astype(BF16), ffn_w_out[j].astype(BF16), xs, norm_g[i, 3], mods,
                               rows, n_lat, g_next, mods_next)
        else:
            xs = _ffn_moe(h2, moe_router[j], moe_w_in[j].astype(BF16), moe_w_out[j].astype(BF16), xs,
                          norm_g[i, 3], mods, rows, n_lat)
            if not last:
                h = _norm_mod(xs, norm_g[i + 1, 0], mods_all[i + 1], n_all, n_lat, K_SHIFT_TOK, K_SCALE_TOK, BF16)
    return xs[:, :n_lat]
```

```python
import functools
import math

import jax
import jax.numpy as jnp
from jax import lax
from jax.experimental import pallas as pl
from jax.experimental.pallas import tpu as pltpu

F32 = jnp.float32
BF16 = jnp.bfloat16

GRID_W = 64
ROPE_THETA = 10000.0
EPS = 1e-6
GN_EPS = 1e-5
A_DIM = 64
B_QK = 128
B_V = 256
C_DIM = 128
C_GROUP = 4
TOP_K = 2
K_SHIFT_TOK, K_SCALE_TOK, K_GATE_TOK, K_SHIFT_CH, K_SCALE_CH, K_GATE_CH = range(6)
LANES = 128
SUBLANES = 8
N_DMA_PRIORITIES = 2
LOG2E = 1.4426950408889634
BOUND_SLACK = 1.02
DEN_FLOOR = 2.0 ** -88
VMEM_LIMIT_BYTES = 56 * 1024 * 1024


def _pick(n, target, mult):
    best = None
    for t in range(mult, min(n, target) + 1, mult):
        if n % t == 0:
            best = t
    return best if best is not None else n


def _params(n_axes):
    return pltpu.CompilerParams(dimension_semantics=("arbitrary",) * n_axes,
                                vmem_limit_bytes=VMEM_LIMIT_BYTES)


def _silu(v):
    return v * jax.nn.sigmoid(v)


def _rms(v, eps):
    return v * lax.rsqrt(jnp.mean(v * v, axis=-1, keepdims=True) + eps)


def _ctx_rows(tile_idx, tm, n_lat, has_ctx):
    if not has_ctx:
        return None
    row = tile_idx * tm + lax.broadcasted_iota(jnp.int32, (tm, 1), 0)
    return row >= n_lat


def _mod_vec(ml_ref, mc_ref, k, is_ctx):
    return ml_ref[k] if is_ctx is None else jnp.where(is_ctx, mc_ref[k], ml_ref[k])


def _modulated_norm(x, g_ref, ml_ref, mc_ref, k_shift, k_scale, is_ctx):
    y = _rms(x, EPS) * g_ref[...]
    return y * (1.0 + _mod_vec(ml_ref, mc_ref, k_scale, is_ctx)) + _mod_vec(ml_ref, mc_ref, k_shift, is_ctx)


def _gated_residual(x, f, g_ref, ml_ref, mc_ref, k_gate, is_ctx):
    return x + _mod_vec(ml_ref, mc_ref, k_gate, is_ctx) * (_rms(f, EPS) * g_ref[...])


def _modvec_kernel(c_ref, w_ref, b_ref, o_ref):
    a = _silu(c_ref[...]).astype(BF16)
    o_ref[...] = jnp.dot(a, w_ref[...].astype(BF16), preferred_element_type=F32) + b_ref[...]


def _modvec(cond, w_mod, b_mod):
    depth, d, n6 = w_mod.shape
    rows = cond.shape[0]
    tn = _pick(n6, 1024, LANES)
    return pl.pallas_call(
        _modvec_kernel,
        grid=(depth, n6 // tn),
        in_specs=[pl.BlockSpec((rows, d), lambda i, n: (0, 0)),
                  pl.BlockSpec((None, d, tn), lambda i, n: (i, 0, n)),
                  pl.BlockSpec((None, 1, tn), lambda i, n: (i, 0, n))],
        out_specs=pl.BlockSpec((None, rows, tn), lambda i, n: (i, 0, n)),
        out_shape=jax.ShapeDtypeStruct((depth, rows, n6), F32),
        compiler_params=_params(2),
    )(cond, w_mod, b_mod.reshape(depth, 1, n6))


def _norm_mod_kernel(x_ref, g_ref, ml_ref, mc_ref, o_ref, *, k_shift, k_scale, n_lat, tm, has_ctx):
    is_ctx = _ctx_rows(pl.program_id(1), tm, n_lat, has_ctx)
    o_ref[...] = _modulated_norm(x_ref[...], g_ref, ml_ref, mc_ref, k_shift, k_scale, is_ctx).astype(o_ref.dtype)


def _norm_mod(x, g, mods, rows, n_lat, k_shift, k_scale, out_dtype):
    n_batch, _, d = x.shape
    tm = _pick(rows, 512, 16)
    kern = functools.partial(_norm_mod_kernel, k_shift=k_shift, k_scale=k_scale, n_lat=n_lat, tm=tm,
                             has_ctx=rows > n_lat)
    return pl.pallas_call(
        kern,
        grid=(n_batch, rows // tm),
        in_specs=[pl.BlockSpec((None, tm, d), lambda b, j: (b, j, 0)),
                  pl.BlockSpec((1, d), lambda b, j: (0, 0)),
                  pl.BlockSpec((None, 6, 1, d), lambda b, j: (b, 0, 0, 0)),
                  pl.BlockSpec((None, 6, 1, d), lambda b, j: (n_batch, 0, 0, 0))],
        out_specs=pl.BlockSpec((None, tm, d), lambda b, j: (b, j, 0)),
        out_shape=jax.ShapeDtypeStruct((n_batch, rows, d), out_dtype),
        compiler_params=_params(2),
    )(x, g.reshape(1, d), mods, mods)


def _rope_tables(n_lat, n_ctx):
    rows = n_lat // GRID_W
    row = jnp.repeat(jnp.arange(rows, dtype=F32), GRID_W)
    col = jnp.tile(jnp.arange(GRID_W, dtype=F32), rows)

    def angles(dim):
        n = dim // 4
        inv = ROPE_THETA ** (-jnp.arange(n, dtype=F32) / n)
        return jnp.concatenate([row[:, None] * inv, col[:, None] * inv], axis=-1)

    a64, a128 = angles(A_DIM), angles(C_DIM)
    lane = jnp.arange(LANES)
    c64, s64 = jnp.cos(a64)[:, lane % 32], jnp.sin(a64)[:, lane % 32]
    first_half = (lane % 64) < 32
    sa64 = jnp.where(first_half, -s64, 0.0)
    sb64 = jnp.where(first_half, 0.0, s64)
    c128 = jnp.cos(a128)[:, lane % 64]
    s128 = jnp.sin(a128)[:, lane % 64] * jnp.where(lane < 64, -1.0, 1.0)
    lat = jnp.stack([c64, sa64, sb64, c128, s128])
    ident = jnp.zeros((5, n_ctx, LANES), F32).at[jnp.array([0, 3])].set(1.0)
    return jnp.concatenate([lat, ident], axis=1)


def _proj_epilogue(kind, x, tab_ref, qkg_ref):
    name = kind[0]
    if name == "none":
        return x
    if name == "silu":
        return _silu(x)
    if name == "rope64":
        y = (x * tab_ref[0] + pltpu.roll(x, 96, 1) * tab_ref[1] + pltpu.roll(x, 32, 1) * tab_ref[2])
        return y * kind[1]
    if name == "norm_rope128":
        x = _rms(x, EPS) * qkg_ref[kind[2]:kind[2] + 1, :]
    y = x * tab_ref[3] + pltpu.roll(x, 64, 1) * tab_ref[4]
    return y * kind[1]


def _proj_kernel(h_ref, w_ref, tab_ref, qkg_ref, o_ref, acc_ref, *, groups, n_tiles):
    n = pl.program_id(2)
    slot = lax.rem(n, 2)

    def matmul():
        acc_ref[slot] = jnp.dot(h_ref[...], w_ref[...], preferred_element_type=F32)

    def epilogue(kinds):
        prev = acc_ref.at[1 - slot]
        for c, kind in enumerate(kinds):
            sl = slice(c * LANES, (c + 1) * LANES)
            o_ref[:, sl] = _proj_epilogue(kind, prev[:, sl], tab_ref, qkg_ref).astype(o_ref.dtype)

    pl.when(n == 0)(matmul)
    for tiles, kinds in groups:
        prev_in_group = functools.reduce(jnp.logical_or, [n - 1 == t for t in tiles])

        @pl.when(prev_in_group & (n < n_tiles))
        def _(kinds=kinds):
            epilogue(kinds)
            matmul()

        if n_tiles - 1 in tiles:
            pl.when(n == n_tiles)(functools.partial(epilogue, kinds))


def _proj_layout(d):
    a_heads, b_heads, c_heads = d // 256, d // 512, d // 256
    c_kv = c_heads // C_GROUP
    segs = [("aq", a_heads, ("rope64", A_DIM ** -0.5 * LOG2E)), ("ak", a_heads, ("rope64", 1.0)),
            ("av", a_heads, ("none",)),
            ("bq", b_heads, ("rope128", 1.0)), ("bk", b_heads, ("rope128", B_QK ** -0.5)),
            ("bv", 2 * b_heads, ("none",)), ("bg", 2 * b_heads, ("silu",)),
            ("cq", c_heads, ("norm_rope128", C_DIM ** -0.5 * LOG2E, 0)), ("ck", c_kv, ("norm_rope128", 1.0, 1)),
            ("cv", c_kv, ("none",))]
    kinds, off = [], {}
    for name, nblk, kind in segs:
        off[name] = len(kinds)
        kinds += [kind] * nblk
    return kinds, off


def _proj(h, w, tabs, qkg, kinds):
    n_batch, rows, d = h.shape
    width = w.shape[1]
    tm = _pick(rows, 1152, 16)
    tn = _pick(width, 512, LANES)
    per = tn // LANES
    by_kinds = {}
    for t in range(width // tn):
        by_kinds.setdefault(tuple(kinds[t * per:(t + 1) * per]), []).append(t)
    groups = tuple((tuple(tiles), ks) for ks, tiles in by_kinds.items())
    n_tiles = width // tn
    return pl.pallas_call(
        functools.partial(_proj_kernel, groups=groups, n_tiles=n_tiles),
        grid=(n_batch, rows // tm, n_tiles + 1),
        in_specs=[pl.BlockSpec((None, tm, d), lambda b, j, n: (b, j, 0)),
                  pl.BlockSpec((d, tn), lambda b, j, n: (0, jnp.minimum(n, n_tiles - 1))),
                  pl.BlockSpec((5, tm, LANES), lambda b, j, n: (0, j, 0)),
                  pl.BlockSpec((2, LANES), lambda b, j, n: (0, 0))],
        out_specs=pl.BlockSpec((None, tm, tn), lambda b, j, n: (b, j, jnp.maximum(n - 1, 0))),
        out_shape=jax.ShapeDtypeStruct((n_batch, rows, width), BF16),
        scratch_shapes=[pltpu.VMEM((2, tm, tn), F32)],
        compiler_params=_params(3),
    )(h, w, tabs, qkg)


def _scores(q, k):
    return lax.dot_general(q, k, (((1,), (1,)), ((), ())), preferred_element_type=F32)


def _softmax_pv_rowmax(q, k, vx):
    s = _scores(q, k)
    e = jnp.exp2(s - jnp.max(s, axis=-1, keepdims=True)).astype(BF16)
    r = jnp.dot(e, vx, preferred_element_type=F32)
    return r[:, :LANES] / r[:, LANES:LANES + 1]


def _bound_shift(q, ksq_max):
    qf = q.astype(F32)
    return BOUND_SLACK * jnp.sqrt(jnp.sum(qf * qf, axis=-1, keepdims=True) * ksq_max)


def _attn_kernel(q_ref, k_ref, v_ref, lam_ref, g_ref, o_ref, vx_ref, ksq_ref, *,
                 diff, lam_init, n_lat_tiles, n_lat, has_ctx, heads, kv_shared):
    i = pl.program_id(2)
    n_kv = 1 if kv_shared else heads
    n_keys = vx_ref.shape[1]

    def lanes(j):
        return slice(j * LANES, (j + 1) * LANES)

    @pl.when(i == 0)
    def _():
        for j in range(n_kv):
            vx_ref[j, :, :LANES] = v_ref[:, lanes(j)]
            vx_ref[j, :, LANES:] = jnp.ones((n_keys, LANES), BF16)
            k = k_ref[:, lanes(j)]
            ksq = jnp.dot(k * k, jnp.ones((LANES, LANES), BF16), preferred_element_type=F32)
            ksq_ref[j] = jnp.broadcast_to(jnp.max(ksq, axis=0, keepdims=True), (SUBLANES, LANES))

    def run(key_rows, nq):
        def operands(hh):
            j = 0 if kv_shared else hh
            q = q_ref[:nq, lanes(hh)]
            if diff:
                qf = q.astype(F32)
                lane = lax.broadcasted_iota(jnp.int32, qf.shape, 1)
                qs = (jnp.where(lane < A_DIM, qf, 0.0).astype(BF16), jnp.where(lane >= A_DIM, qf, 0.0).astype(BF16))
            else:
                qs = (q,)
            return qs, k_ref[key_rows, lanes(j)], vx_ref[j, key_rows, :], ksq_ref[j, 0:1, 0:1]

        def finish(hh, outs):
            if diff:
                lv = lam_ref[...]
                lam = (jnp.exp(jnp.sum(lv[0:1] * lv[1:2], axis=-1, keepdims=True))
                       - jnp.exp(jnp.sum(lv[2:3] * lv[3:4], axis=-1, keepdims=True)) + lam_init)
                o = _rms(outs[0] - lam * outs[1], EPS) * g_ref[...] * (1.0 - lam_init)
            else:
                o = outs[0]
            o_ref[:nq, lanes(hh)] = o.astype(o_ref.dtype)

        ops = [(hh, qq, k, vx, ksq) for hh in range(heads) for qs, k, vx, ksq in [operands(hh)] for qq in qs]
        shifted = [_scores(qq, k) - _bound_shift(qq, ksq) for _, qq, k, _, ksq in ops]
        exps = [jnp.exp2(s).astype(BF16) for s in shifted]
        prods = [jnp.dot(e, op[3], preferred_element_type=F32) for e, op in zip(exps, ops)]
        parts = [[] for _ in range(heads)]
        for op, r in zip(ops, prods):
            parts[op[0]].append((r[:, :LANES], r[:, LANES:LANES + 1]))
        den_min = functools.reduce(jnp.minimum, [den for head in parts for _, den in head])
        precise = jnp.min(den_min) >= DEN_FLOOR

        @pl.when(precise)
        def _():
            for hh in range(heads):
                finish(hh, [num / den for num, den in parts[hh]])

        @pl.when(jnp.logical_not(precise))
        def _():
            for hh in range(heads):
                qs, k, vx, _ = operands(hh)
                finish(hh, [_softmax_pv_rowmax(qq, k, vx) for qq in qs])

    tq = q_ref.shape[0]
    if not has_ctx:
        run(slice(None), tq)
        return

    @pl.when(i < n_lat_tiles)
    def _():
        run(slice(None), tq)

    @pl.when(i >= n_lat_tiles)
    def _():
        run(slice(n_lat, None), min(tq, n_keys - n_lat))


def _query_tile(n_lat, n_ctx, has_ctx):
    tq = _pick(n_lat, 512, 16)
    assert not has_ctx or n_ctx % tq == 0 or n_ctx < tq
    return tq


def _attention(p, lam_vec, subln_g, rows, n_lat, n_ctx, n_heads, q_off, k_off, v_off, kv_group, diff, lam_init):
    n_batch, n_keys, _ = p.shape
    has_ctx = rows > n_lat
    tq = _query_tile(n_lat, n_ctx, has_ctx)
    kv_shared = kv_group > 1
    heads = kv_group if kv_shared else 2
    assert n_heads % heads == 0 and q_off % heads == 0 and (kv_shared or (k_off % heads == 0 and v_off % heads == 0))
    kv_w = LANES if kv_shared else heads * LANES
    kv_blk = (lambda off: (lambda b, h, i: (b, 0, off + h))) if kv_shared else \
             (lambda off: (lambda b, h, i: (b, 0, off // heads + h)))
    kern = functools.partial(_attn_kernel, diff=diff, lam_init=lam_init, n_lat_tiles=n_lat // tq,
                             n_lat=n_lat, has_ctx=has_ctx, heads=heads, kv_shared=kv_shared)
    n_kv = 1 if kv_shared else heads
    return pl.pallas_call(
        kern,
        grid=(n_batch, n_heads // heads, pl.cdiv(rows, tq)),
        in_specs=[pl.BlockSpec((None, tq, heads * LANES), lambda b, h, i: (b, i, q_off // heads + h)),
                  pl.BlockSpec((None, n_keys, kv_w), kv_blk(k_off)),
                  pl.BlockSpec((None, n_keys, kv_w), kv_blk(v_off)),
                  pl.BlockSpec(lam_vec.shape, lambda b, h, i: (0, 0)),
                  pl.BlockSpec((1, LANES), lambda b, h, i: (0, 0))],
        out_specs=pl.BlockSpec((None, tq, heads * LANES), lambda b, h, i: (b, i, h)),
        out_shape=jax.ShapeDtypeStruct((n_batch, rows, n_heads * LANES), BF16),
        scratch_shapes=[pltpu.VMEM((n_kv, n_keys, 2 * LANES), BF16), pltpu.VMEM((n_kv, SUBLANES, LANES), F32)],
        compiler_params=_params(3),
    )(p, p, p, lam_vec, subln_g.reshape(1, LANES))


def _ret_decay(lgf, lgb, t, pf, pb):
    df = (t - pf).astype(F32)
    db = (pb - t).astype(F32)
    fwd = jnp.where(df >= 0, jnp.exp(lgf * jnp.maximum(df, 0.0)), 0.0)
    bwd = jnp.where(db >= 0, jnp.exp(lgb * jnp.maximum(db, 0.0)), 0.0)
    return fwd + bwd


def _ret_kernel(ld_ref, q_ref, k_ref, v_ref, g_ref, o_ref, d_ref, *, tq, n_keys, n_lat, n_heads, has_ctx, heads):
    hp, i, b = pl.program_id(0), pl.program_id(1), pl.program_id(2)
    n_ctx = n_keys - n_lat
    n_lat_tiles = n_lat // tq
    t_loc = lax.broadcasted_iota(jnp.int32, (tq, 1), 0)

    def log_decays(j):
        return ld_ref[hp * heads + j], ld_ref[n_heads + hp * heads + j]

    def lat_decay():
        col = lax.broadcasted_iota(jnp.int32, (1, n_keys), 1)
        pf = jnp.where(col < n_lat, col, col - n_keys)
        for j in range(heads):
            d_ref[j] = _ret_decay(*log_decays(j), i * tq + t_loc, pf, col)

    def ctx_decay():
        col = lax.broadcasted_iota(jnp.int32, (1, n_ctx), 1)
        for j in range(heads):
            d_ref[j, :, :n_ctx] = _ret_decay(*log_decays(j), (i - n_lat_tiles) * tq + t_loc, col, col)

    def run(key_rows, n_cols, nq):
        def qk(j):
            sl = slice(j * B_QK, (j + 1) * B_QK)
            return lax.dot_general(q_ref[:nq, sl], k_ref[key_rows, sl], (((1,), (1,)), ((), ())),
                                   preferred_element_type=F32)

        scores = [qk(j) for j in range(heads)]
        weighted = [(s * d_ref[j, :nq, :n_cols]).astype(BF16) for j, s in enumerate(scores)]
        for j, w in enumerate(weighted):
            sl = slice(j * B_V, (j + 1) * B_V)
            o = jnp.dot(w, v_ref[key_rows, sl], preferred_element_type=F32)
            mu = jnp.mean(o, axis=-1, keepdims=True)
            var = jnp.mean(jnp.square(o - mu), axis=-1, keepdims=True)
            o_ref[:nq, sl] = ((o - mu) * lax.rsqrt(var + GN_EPS) * g_ref[:nq, sl].astype(F32)).astype(o_ref.dtype)

    if not has_ctx:
        pl.when(b == 0)(lat_decay)
        run(slice(None), n_keys, tq)
        return

    pl.when((b == 0) & (i < n_lat_tiles))(lat_decay)
    pl.when((b == 0) & (i >= n_lat_tiles))(ctx_decay)

    @pl.when(i < n_lat_tiles)
    def _():
        run(slice(None), n_keys, tq)

    @pl.when(i >= n_lat_tiles)
    def _():
        run(slice(n_lat, None), n_ctx, min(tq, n_ctx))


def _retention(p, log_decay, rows, n_lat, n_ctx, n_heads, off):
    n_batch, n_keys, _ = p.shape
    has_ctx = rows > n_lat
    tq = _query_tile(n_lat, n_ctx, has_ctx)
    heads = 2 if n_heads % 2 == 0 else 1
    v_off, g_off = off["bv"] // 2, off["bg"] // 2
    assert all(o % heads == 0 for o in (off["bq"], off["bk"], v_off, g_off))
    kern = functools.partial(_ret_kernel, tq=tq, n_keys=n_keys, n_lat=n_lat, n_heads=n_heads, has_ctx=has_ctx,
                             heads=heads)
    grid_spec = pltpu.PrefetchScalarGridSpec(
        num_scalar_prefetch=1,
        grid=(n_heads // heads, pl.cdiv(rows, tq), n_batch),
        in_specs=[pl.BlockSpec((None, tq, heads * B_QK), lambda h, i, b, ld: (b, i, off["bq"] // heads + h)),
                  pl.BlockSpec((None, n_keys, heads * B_QK), lambda h, i, b, ld: (b, 0, off["bk"] // heads + h)),
                  pl.BlockSpec((None, n_keys, heads * B_V), lambda h, i, b, ld: (b, 0, v_off // heads + h)),
                  pl.BlockSpec((None, tq, heads * B_V), lambda h, i, b, ld: (b, i, g_off // heads + h))],
        out_specs=pl.BlockSpec((None, tq, heads * B_V), lambda h, i, b, ld: (b, i, h)),
        scratch_shapes=[pltpu.VMEM((heads, tq, n_keys), F32)])
    return pl.pallas_call(
        kern, grid_spec=grid_spec,
        out_shape=jax.ShapeDtypeStruct((n_batch, rows, n_heads * B_V), BF16),
        compiler_params=_params(3),
    )(log_decay.reshape(-1).astype(F32), p, p, p, p)


def _merge_kernel(h_ref, oa_ref, ob_ref, oc_ref, ga_ref, gb_ref, gc_ref, wb_ref, y_ref):
    h = h_ref[...]
    y = None
    for idx, (o_ref, wg_ref) in enumerate(((oa_ref, ga_ref), (ob_ref, gb_ref), (oc_ref, gc_ref))):
        gate = jax.nn.sigmoid(jnp.dot(h, wg_ref[...], preferred_element_type=F32))
        term = gate * jnp.dot(o_ref[...], wb_ref[idx], preferred_element_type=F32)
        y = term if y is None else y + term
    y_ref[...] = y.astype(y_ref.dtype)


def _merge(h, outs, w_gate, w_branch, rows):
    n_batch, _, d = h.shape
    bw = w_branch.shape[1]
    tm = _pick(rows, 1152, 16)
    tn = _pick(d, 512, LANES)
    nn = d // tn
    o_spec = pl.BlockSpec((None, tm, bw), lambda b, j, n: (b, j, 0))
    g_specs = [pl.BlockSpec((d, tn), functools.partial(lambda b, j, n, k: (0, k * nn + n), k=k)) for k in range(3)]
    return pl.pallas_call(
        _merge_kernel,
        grid=(n_batch, rows // tm, nn),
        in_specs=[pl.BlockSpec((None, tm, d), lambda b, j, n: (b, j, 0)), o_spec, o_spec, o_spec,
                  *g_specs, pl.BlockSpec((3, bw, tn), lambda b, j, n: (0, 0, n))],
        out_specs=pl.BlockSpec((None, tm, tn), lambda b, j, n: (b, j, n)),
        out_shape=jax.ShapeDtypeStruct((n_batch, rows, d), BF16),
        compiler_params=_params(3),
    )(h, *outs, w_gate, w_gate, w_gate, w_branch)


def _out_resid_kernel(y_ref, w_ref, x_ref, g_ref, gn_ref, ml_ref, mc_ref, o_ref, hn_ref, *, tm, n_lat, has_ctx):
    is_ctx = _ctx_rows(pl.program_id(1), tm, n_lat, has_ctx)
    f = jnp.dot(y_ref[...], w_ref[...], preferred_element_type=F32)
    x_new = _gated_residual(x_ref[...], f, g_ref, ml_ref, mc_ref, K_GATE_TOK, is_ctx)
    o_ref[...] = x_new
    hn_ref[...] = _modulated_norm(x_new, gn_ref, ml_ref, mc_ref, K_SHIFT_CH, K_SCALE_CH, is_ctx).astype(hn_ref.dtype)


def _out_resid(y, w_out, x, g, g_next, mods, rows, n_lat, next_dtype):
    n_batch, _, d = y.shape
    tm = _pick(rows, 512, 16)
    kern = functools.partial(_out_resid_kernel, tm=tm, n_lat=n_lat, has_ctx=rows > n_lat)
    row_spec = pl.BlockSpec((None, tm, d), lambda b, j: (b, j, 0))
    vec_spec = pl.BlockSpec((1, d), lambda b, j: (0, 0))
    return pl.pallas_call(
        kern,
        grid=(n_batch, rows // tm),
        in_specs=[row_spec, pl.BlockSpec((d, d), lambda b, j: (0, 0)), row_spec, vec_spec, vec_spec,
                  pl.BlockSpec((None, 6, 1, d), lambda b, j: (b, 0, 0, 0)),
                  pl.BlockSpec((None, 6, 1, d), lambda b, j: (n_batch, 0, 0, 0))],
        out_specs=[row_spec, row_spec],
        out_shape=[jax.ShapeDtypeStruct((n_batch, rows, d), F32), jax.ShapeDtypeStruct((n_batch, rows, d), next_dtype)],
        compiler_params=_params(2),
    )(y, w_out, x, g.reshape(1, d), g_next.reshape(1, d), mods, mods)


def _ffn_kernel(*refs, tm, n_lat, has_ctx, has_next):
    if has_next:
        (h_ref, wg_ref, wu_ref, wo_ref, x_ref, g_ref, ml_ref, mc_ref, gn_ref, mln_ref, mcn_ref,
         o_ref, hn_ref) = refs
    else:
        h_ref, wg_ref, wu_ref, wo_ref, x_ref, g_ref, ml_ref, mc_ref, o_ref = refs
    f = pl.program_id(2)

    @pl.when(f == 0)
    def _():
        o_ref[...] = jnp.zeros_like(o_ref)

    h = h_ref[...]
    gate = jnp.dot(h, wg_ref[...], preferred_element_type=F32)
    up = jnp.dot(h, wu_ref[...], preferred_element_type=F32)
    o_ref[...] += jnp.dot((_silu(gate) * up).astype(BF16), wo_ref[...], preferred_element_type=F32)

    @pl.when(f == pl.num_programs(2) - 1)
    def _():
        is_ctx = _ctx_rows(pl.program_id(1), tm, n_lat, has_ctx)
        x_new = _gated_residual(x_ref[...], o_ref[...], g_ref, ml_ref, mc_ref, K_GATE_CH, is_ctx)
        o_ref[...] = x_new
        if has_next:
            hn_ref[...] = _modulated_norm(x_new, gn_ref, mln_ref, mcn_ref, K_SHIFT_TOK, K_SCALE_TOK,
                                          is_ctx).astype(hn_ref.dtype)


def _ffn_dense(h, w_in, w_out, x, g, mods, rows, n_lat, g_next, mods_next):
    n_batch, _, d = h.shape
    ff = w_out.shape[0]
    tm = _pick(rows, 576, 16)
    tf = _pick(ff, 512, LANES)
    nf = ff // tf
    has_next = g_next is not None
    kern = functools.partial(_ffn_kernel, tm=tm, n_lat=n_lat, has_ctx=rows > n_lat, has_next=has_next)
    row_spec = pl.BlockSpec((None, tm, d), lambda b, j, f: (b, j, 0))
    vec_spec = pl.BlockSpec((1, d), lambda b, j, f: (0, 0))
    mod_specs = [pl.BlockSpec((None, 6, 1, d), lambda b, j, f: (b, 0, 0, 0)),
                 pl.BlockSpec((None, 6, 1, d), lambda b, j, f: (n_batch, 0, 0, 0))]
    in_specs = [row_spec,
                pl.BlockSpec((d, tf), lambda b, j, f: (0, f)),
                pl.BlockSpec((d, tf), lambda b, j, f: (0, nf + f)),
                pl.BlockSpec((tf, d), lambda b, j, f: (f, 0)),
                row_spec, vec_spec, *mod_specs]
    args = [h, w_in, w_in, w_out, x, g.reshape(1, d), mods, mods]
    out_specs, out_shape = [row_spec], [jax.ShapeDtypeStruct((n_batch, rows, d), F32)]
    if has_next:
        in_specs += [vec_spec, *mod_specs]
        args += [g_next.reshape(1, d), mods_next, mods_next]
        out_specs.append(row_spec)
        out_shape.append(jax.ShapeDtypeStruct((n_batch, rows, d), BF16))
    outs = pl.pallas_call(
        kern,
        grid=(n_batch, rows // tm, nf),
        in_specs=in_specs, out_specs=out_specs, out_shape=out_shape,
        compiler_params=_params(3),
    )(*args)
    return (outs[0], outs[1]) if has_next else (outs[0], None)


def _router_kernel(h_ref, w_ref, idx_ref, wt_ref, *, n_experts):
    logits = jnp.dot(h_ref[...].astype(BF16), w_ref[...], preferred_element_type=F32)
    lane = lax.broadcasted_iota(jnp.int32, logits.shape, 1)
    neg = jnp.float32(-jnp.inf)
    lg = jnp.where(lane < n_experts, logits, neg)
    m1 = jnp.max(lg, axis=-1, keepdims=True)
    i1 = jnp.min(jnp.where(lg == m1, lane, LANES), axis=-1, keepdims=True)
    lg2 = jnp.where(lane == i1, neg, lg)
    m2 = jnp.max(lg2, axis=-1, keepdims=True)
    i2 = jnp.min(jnp.where(lg2 == m2, lane, LANES), axis=-1, keepdims=True)
    e2 = jnp.exp(m2 - m1)
    den = 1.0 + e2
    idx_ref[...] = jnp.where(lane == 0, i1, jnp.where(lane == 1, i2, 0))
    wt_ref[...] = jnp.where(lane == 0, 1.0 / den, jnp.where(lane == 1, e2 / den, 0.0))


def _router(h, w_router):
    n_tok, d = h.shape
    n_experts = w_router.shape[1]
    tm = _pick(n_tok, 1024, 8)
    w_pad = jnp.zeros((d, LANES), BF16).at[:, :n_experts].set(w_router.astype(BF16))
    out_spec = pl.BlockSpec((tm, LANES), lambda i: (i, 0))
    return pl.pallas_call(
        functools.partial(_router_kernel, n_experts=n_experts),
        grid=(n_tok // tm,),
        in_specs=[pl.BlockSpec((tm, d), lambda i: (i, 0)), pl.BlockSpec((d, LANES), lambda i: (0, 0))],
        out_specs=[out_spec, out_spec],
        out_shape=[jax.ShapeDtypeStruct((n_tok, LANES), jnp.int32), jax.ShapeDtypeStruct((n_tok, LANES), F32)],
        compiler_params=_params(1),
    )(h, w_pad)


def _route_plan(top_idx, n_experts, tm, n_tiles):
    n_tok = top_idx.shape[0]
    e_flat = top_idx.reshape(-1)
    onehot = (e_flat[:, None] == jnp.arange(n_experts, dtype=jnp.int32)[None, :]).astype(jnp.int32)
    csum = jnp.cumsum(onehot, axis=0)
    rank = jnp.sum((csum - onehot) * onehot, axis=1)
    tiles_e = (csum[-1] + tm - 1) // tm
    tiles_end = jnp.cumsum(tiles_e)
    pos = (tiles_end - tiles_e)[e_flat] * tm + rank
    n_used = tiles_end[-1]
    tile_ids = jnp.arange(n_tiles, dtype=jnp.int32)
    tile_expert = jnp.sum((tile_ids[:, None] >= tiles_end[None, :]).astype(jnp.int32), axis=1)
    tile_expert = jnp.minimum(tile_expert, n_experts - 1)
    last_expert = tile_expert[jnp.maximum(n_used - 1, 0)]
    tile_expert = jnp.where(tile_ids < n_used, tile_expert, last_expert)
    row_token = jnp.zeros((n_tiles * tm,), jnp.int32).at[pos].set(jnp.arange(TOP_K * n_tok, dtype=jnp.int32) // TOP_K)
    return row_token, pos.reshape(n_tok, TOP_K).astype(jnp.int32), tile_expert, n_used.reshape(1).astype(jnp.int32)


def _gather_kernel(tok_ref, src_ref, o_ref, buf_ref, sem, *, tg):
    def issue(pair, carry):
        for u in range(N_DMA_PRIORITIES):
            r = N_DMA_PRIORITIES * pair + u
            pltpu.make_async_copy(src_ref.at[pl.ds(tok_ref[0, r], 1)], buf_ref.at[pl.ds(r, 1)],
                                  sem).start(priority=u)
        return carry

    def drain(r, carry):
        pltpu.make_async_copy(src_ref.at[pl.ds(0, 1)], buf_ref.at[pl.ds(r, 1)], sem).wait()
        return carry

    lax.fori_loop(0, tg // N_DMA_PRIORITIES, issue, 0, unroll=4)
    lax.fori_loop(0, tg, drain, 0, unroll=8)
    o_ref[...] = buf_ref[...].astype(o_ref.dtype)


def _gather_rows(src, row_token, tg, out_dtype):
    n_rows = row_token.shape[0]
    d = src.shape[1]
    return pl.pallas_call(
        functools.partial(_gather_kernel, tg=tg),
        grid=(n_rows // tg,),
        in_specs=[pl.BlockSpec((None, 1, tg), lambda i: (i, 0, 0), memory_space=pltpu.SMEM),
                  pl.BlockSpec(memory_space=pl.ANY)],
        out_specs=pl.BlockSpec((tg, d), lambda i: (i, 0)),
        out_shape=jax.ShapeDtypeStruct((n_rows, d), out_dtype),
        scratch_shapes=[pltpu.VMEM((tg, d), src.dtype), pltpu.SemaphoreType.DMA(())],
        compiler_params=_params(1),
    )(row_token.reshape(n_rows // tg, 1, tg), src)


def _moe_ffn_kernel(te_ref, nv_ref, xs_ref, wg_ref, wu_ref, wo_ref, o_ref):
    i, f = pl.program_id(0), pl.program_id(1)
    valid = i < nv_ref[0]

    @pl.when(f == 0)
    def _():
        o_ref[...] = jnp.zeros_like(o_ref)

    @pl.when(valid)
    def _():
        xb = xs_ref[...]
        gate = jnp.dot(xb, wg_ref[...], preferred_element_type=F32)
        up = jnp.dot(xb, wu_ref[...], preferred_element_type=F32)
        o_ref[...] += jnp.dot((_silu(gate) * up).astype(BF16), wo_ref[...], preferred_element_type=F32)


def _moe_ffn(xs, w_in, w_out, tile_expert, n_used, tm):
    n_rows, d = xs.shape
    ff = w_out.shape[1]
    tf = _pick(ff, 512, LANES)
    nf = ff // tf

    def f_eff(i, f, nv):
        return jnp.where(i < nv[0], f, nf - 1)

    grid_spec = pltpu.PrefetchScalarGridSpec(
        num_scalar_prefetch=2,
        grid=(n_rows // tm, nf),
        in_specs=[pl.BlockSpec((tm, d), lambda i, f, te, nv: (jnp.minimum(i, jnp.maximum(nv[0] - 1, 0)), 0)),
                  pl.BlockSpec((None, d, tf), lambda i, f, te, nv: (te[i], 0, f_eff(i, f, nv))),
                  pl.BlockSpec((None, d, tf), lambda i, f, te, nv: (te[i], 0, nf + f_eff(i, f, nv))),
                  pl.BlockSpec((None, tf, d), lambda i, f, te, nv: (te[i], f_eff(i, f, nv), 0))],
        out_specs=pl.BlockSpec((tm, d), lambda i, f, te, nv: (i, 0)))
    return pl.pallas_call(
        _moe_ffn_kernel, grid_spec=grid_spec,
        out_shape=jax.ShapeDtypeStruct((n_rows, d), F32),
        compiler_params=_params(2),
    )(tile_expert, n_used, xs, w_in, w_in, w_out)


def _combine_kernel(pos_ref, ys_ref, wt_ref, x_ref, g_ref, ml_ref, mc_ref, o_ref, buf_ref, sem, *,
                    tc, n_lat, has_ctx):
    def issue(r, carry):
        for k in range(TOP_K):
            pltpu.make_async_copy(ys_ref.at[pl.ds(pos_ref[0, TOP_K * r + k], 1)],
                                  buf_ref.at[k, pl.ds(r, 1)], sem).start(priority=k % N_DMA_PRIORITIES)
        return carry

    def drain(r, carry):
        for k in range(TOP_K):
            pltpu.make_async_copy(ys_ref.at[pl.ds(0, 1)], buf_ref.at[k, pl.ds(r, 1)], sem).wait()
        return carry

    lax.fori_loop(0, tc, issue, 0)
    lax.fori_loop(0, tc, drain, 0)
    wt = wt_ref[...]
    f = wt[:, 0:1] * buf_ref[0]
    for k in range(1, TOP_K):
        f = f + wt[:, k:k + 1] * buf_ref[k]
    is_ctx = _ctx_rows(pl.program_id(1), tc, n_lat, has_ctx)
    o_ref[...] = _gated_residual(x_ref[...], f, g_ref, ml_ref, mc_ref, K_GATE_CH, is_ctx)


def _combine(ys, pos, wts, x, g, mods, rows, n_lat):
    n_batch, _, d = x.shape
    tc = _pick(rows, 256, 8)
    nj = rows // tc
    kern = functools.partial(_combine_kernel, tc=tc, n_lat=n_lat, has_ctx=rows > n_lat)
    row_spec = pl.BlockSpec((None, tc, d), lambda b, j: (b, j, 0))
    return pl.pallas_call(
        kern,
        grid=(n_batch, nj),
        in_specs=[pl.BlockSpec((None, 1, TOP_K * tc), lambda b, j: (b * nj + j, 0, 0), memory_space=pltpu.SMEM),
                  pl.BlockSpec(memory_space=pl.ANY),
                  pl.BlockSpec((tc, LANES), lambda b, j: (b * nj + j, 0)),
                  row_spec,
                  pl.BlockSpec((1, d), lambda b, j: (0, 0)),
                  pl.BlockSpec((None, 6, 1, d), lambda b, j: (b, 0, 0, 0)),
                  pl.BlockSpec((None, 6, 1, d), lambda b, j: (n_batch, 0, 0, 0))],
        out_specs=row_spec,
        out_shape=jax.ShapeDtypeStruct((n_batch, rows, d), F32),
        scratch_shapes=[pltpu.VMEM((TOP_K, tc, d), F32), pltpu.SemaphoreType.DMA(())],
        compiler_params=_params(2),
    )(pos.reshape(n_batch * nj, 1, TOP_K * tc), ys, wts, x, g.reshape(1, d), mods, mods)


def _ffn_moe(h, w_router, w_in, w_out, x, g, mods, rows, n_lat):
    n_batch, _, d = h.shape
    n_tok = n_batch * rows
    n_experts = w_router.shape[1]
    tm = min(768, max(16, (TOP_K * n_tok // n_experts) // 16 * 16))
    n_tiles = (TOP_K * n_tok + n_experts * (tm - 1)) // tm
    h_flat = h.reshape(n_tok, d)
    idx, wts = _router(h_flat, w_router)
    row_token, pos, tile_expert, n_used = _route_plan(idx[:, :TOP_K], n_experts, tm, n_tiles)
    xs = _gather_rows(h_flat, row_token, tm, BF16)
    ys = _moe_ffn(xs, w_in, w_out, tile_expert, n_used, tm)
    return _combine(ys, pos, wts, x, g, mods, rows, n_lat)


def kernel(x, c, ctx, c_ctx, w_mod, b_mod, norm_g, w_in, qk_norm_g, diff_lambda, diff_subln_g, ret_log_decay,
           w_branch, w_branch_gate, w_out, ffn_w_in, ffn_w_out, moe_router, moe_w_in, moe_w_out):
    n_batch, n_lat, d = x.shape
    n_ctx = ctx.shape[1]
    depth = w_mod.shape[0]
    a_heads, b_heads, c_heads = d // 256, d // 512, d // 256
    kinds, off = _proj_layout(d)

    mod_rows = (n_batch + 1 + 7) // 8 * 8
    cond = jnp.zeros((mod_rows, d), F32).at[:n_batch].set(c).at[n_batch].set(c_ctx)
    mods_all = _modvec(cond, w_mod, b_mod).reshape(depth, mod_rows, 6, 1, d)
    tabs = _rope_tables(n_lat, n_ctx)

    xs = jnp.concatenate([x, ctx], axis=1)
    n_all = n_lat + n_ctx
    h = _norm_mod(xs, norm_g[0, 0], mods_all[0], n_all, n_lat, K_SHIFT_TOK, K_SCALE_TOK, BF16)
    for i in range(depth):
        last = i == depth - 1
        rows = n_lat if last else n_all
        lam_init = 0.8 - 0.6 * math.exp(-0.3 * i)
        mods = mods_all[i]
        dense = i % 2 == 0
        j = i // 2
        p = _proj(h, w_in[i].astype(BF16), tabs, qk_norm_g[i], kinds)
        o_a = _attention(p, diff_lambda[i], diff_subln_g[i], rows, n_lat, n_ctx, a_heads,
                         off["aq"], off["ak"], off["av"], 1, True, lam_init)
        o_b = _retention(p, ret_log_decay[i], rows, n_lat, n_ctx, b_heads, off)
        o_c = _attention(p, diff_lambda[i], diff_subln_g[i], rows, n_lat, n_ctx, c_heads,
                         off["cq"], off["ck"], off["cv"], C_GROUP, False, lam_init)
        y = _merge(h, (o_a, o_b, o_c), w_branch_gate[i].astype(BF16), w_branch[i].astype(BF16), rows)
        xs, h2 = _out_resid(y, w_out[i].astype(BF16), xs, norm_g[i, 1], norm_g[i, 2], mods, rows, n_lat,
                            BF16 if dense else F32)
        if dense:
            g_next, mods_next = (None, None) if last else (norm_g[i + 1, 0], mods_all[i + 1])
            xs, h = _ffn_dense(h2, ffn_w_in[j].astype(BF16), ffn_w_out[j].astype(BF16), xs, norm_g[i, 3], mods,
                               rows, n_lat, g_next, mods_next)
        else:
            xs = _ffn_moe(h2, moe_router[j], moe_w_in[j].astype(BF16), moe_w_out[j].astype(BF16), xs,
                          norm_g[i, 3], mods, rows, n_lat)
            if not last:
                h = _norm_mod(xs, norm_g[i + 1, 0], mods_all[i + 1], n_all, n_lat, K_SHIFT_TOK, K_SCALE_TOK, BF16)
    return xs[:, :n_lat]
```

```python
import functools
import math

import jax
import jax.numpy as jnp
from jax import lax
from jax.experimental import pallas as pl
from jax.experimental.pallas import tpu as pltpu

F32 = jnp.float32
BF16 = jnp.bfloat16

GRID_W = 64
ROPE_THETA = 10000.0
EPS = 1e-6
GN_EPS = 1e-5
A_DIM = 64
B_QK = 128
B_V = 256
C_DIM = 128
C_GROUP = 4
TOP_K = 2
K_SHIFT_TOK, K_SCALE_TOK, K_GATE_TOK, K_SHIFT_CH, K_SCALE_CH, K_GATE_CH = range(6)
LANES = 128
SUBLANES = 8
LOG2E = 1.4426950408889634
BOUND_SLACK = 1.02
DEN_FLOOR = 2.0 ** -88
VMEM_LIMIT_BYTES = 56 * 1024 * 1024


def _pick(n, target, mult):
    best = None
    for t in range(mult, min(n, target) + 1, mult):
        if n % t == 0:
            best = t
    return best if best is not None else n


def _params(n_axes):
    return pltpu.CompilerParams(dimension_semantics=("arbitrary",) * n_axes,
                                vmem_limit_bytes=VMEM_LIMIT_BYTES)


def _silu(v):
    return v * jax.nn.sigmoid(v)


def _rms(v, eps):
    return v * lax.rsqrt(jnp.mean(v * v, axis=-1, keepdims=True) + eps)


def _ctx_rows(tile_idx, tm, n_lat, has_ctx):
    if not has_ctx:
        return None
    row = tile_idx * tm + lax.broadcasted_iota(jnp.int32, (tm, 1), 0)
    return row >= n_lat


def _mod_vec(ml_ref, mc_ref, k, is_ctx):
    return ml_ref[k] if is_ctx is None else jnp.where(is_ctx, mc_ref[k], ml_ref[k])


def _modulated_norm(x, g_ref, ml_ref, mc_ref, k_shift, k_scale, is_ctx):
    y = _rms(x, EPS) * g_ref[...]
    return y * (1.0 + _mod_vec(ml_ref, mc_ref, k_scale, is_ctx)) + _mod_vec(ml_ref, mc_ref, k_shift, is_ctx)


def _gated_residual(x, f, g_ref, ml_ref, mc_ref, k_gate, is_ctx):
    return x + _mod_vec(ml_ref, mc_ref, k_gate, is_ctx) * (_rms(f, EPS) * g_ref[...])


def _modvec_kernel(c_ref, w_ref, b_ref, o_ref):
    a = _silu(c_ref[...]).astype(BF16)
    o_ref[...] = jnp.dot(a, w_ref[...].astype(BF16), preferred_element_type=F32) + b_ref[...]


def _modvec(cond, w_mod, b_mod):
    depth, d, n6 = w_mod.shape
    rows = cond.shape[0]
    tn = _pick(n6, 1024, LANES)
    return pl.pallas_call(
        _modvec_kernel,
        grid=(depth, n6 // tn),
        in_specs=[pl.BlockSpec((rows, d), lambda i, n: (0, 0)),
                  pl.BlockSpec((None, d, tn), lambda i, n: (i, 0, n)),
                  pl.BlockSpec((None, 1, tn), lambda i, n: (i, 0, n))],
        out_specs=pl.BlockSpec((None, rows, tn), lambda i, n: (i, 0, n)),
        out_shape=jax.ShapeDtypeStruct((depth, rows, n6), F32),
        compiler_params=_params(2),
    )(cond, w_mod, b_mod.reshape(depth, 1, n6))


def _norm_mod_kernel(x_ref, g_ref, ml_ref, mc_ref, o_ref, *, k_shift, k_scale, n_lat, tm, has_ctx):
    is_ctx = _ctx_rows(pl.program_id(1), tm, n_lat, has_ctx)
    o_ref[...] = _modulated_norm(x_ref[...], g_ref, ml_ref, mc_ref, k_shift, k_scale, is_ctx).astype(o_ref.dtype)


def _norm_mod(x, g, mods, rows, n_lat, k_shift, k_scale, out_dtype):
    n_batch, _, d = x.shape
    tm = _pick(rows, 512, 16)
    kern = functools.partial(_norm_mod_kernel, k_shift=k_shift, k_scale=k_scale, n_lat=n_lat, tm=tm,
                             has_ctx=rows > n_lat)
    return pl.pallas_call(
        kern,
        grid=(n_batch, rows // tm),
        in_specs=[pl.BlockSpec((None, tm, d), lambda b, j: (b, j, 0)),
                  pl.BlockSpec((1, d), lambda b, j: (0, 0)),
                  pl.BlockSpec((None, 6, 1, d), lambda b, j: (b, 0, 0, 0)),
                  pl.BlockSpec((None, 6, 1, d), lambda b, j: (n_batch, 0, 0, 0))],
        out_specs=pl.BlockSpec((None, tm, d), lambda b, j: (b, j, 0)),
        out_shape=jax.ShapeDtypeStruct((n_batch, rows, d), out_dtype),
        compiler_params=_params(2),
    )(x, g.reshape(1, d), mods, mods)


def _rope_tables(n_lat, n_ctx):
    rows = n_lat // GRID_W
    row = jnp.repeat(jnp.arange(rows, dtype=F32), GRID_W)
    col = jnp.tile(jnp.arange(GRID_W, dtype=F32), rows)

    def angles(dim):
        n = dim // 4
        inv = ROPE_THETA ** (-jnp.arange(n, dtype=F32) / n)
        return jnp.concatenate([row[:, None] * inv, col[:, None] * inv], axis=-1)

    a64, a128 = angles(A_DIM), angles(C_DIM)
    lane = jnp.arange(LANES)
    c64, s64 = jnp.cos(a64)[:, lane % 32], jnp.sin(a64)[:, lane % 32]
    first_half = (lane % 64) < 32
    sa64 = jnp.where(first_half, -s64, 0.0)
    sb64 = jnp.where(first_half, 0.0, s64)
    c128 = jnp.cos(a128)[:, lane % 64]
    s128 = jnp.sin(a128)[:, lane % 64] * jnp.where(lane < 64, -1.0, 1.0)
    lat = jnp.stack([c64, sa64, sb64, c128, s128])
    ident = jnp.zeros((5, n_ctx, LANES), F32).at[jnp.array([0, 3])].set(1.0)
    return jnp.concatenate([lat, ident], axis=1)


def _proj_epilogue(kind, x, tab_ref, qkg_ref):
    name = kind[0]
    if name == "none":
        return x
    if name == "silu":
        return _silu(x)
    if name == "rope64":
        y = (x * tab_ref[0] + pltpu.roll(x, 96, 1) * tab_ref[1] + pltpu.roll(x, 32, 1) * tab_ref[2])
        return y * kind[1]
    if name == "norm_rope128":
        x = _rms(x, EPS) * qkg_ref[kind[2]:kind[2] + 1, :]
    y = x * tab_ref[3] + pltpu.roll(x, 64, 1) * tab_ref[4]
    return y * kind[1]


def _proj_kernel(h_ref, w_ref, tab_ref, qkg_ref, o_ref, acc_ref, *, groups, n_tiles):
    n = pl.program_id(2)
    slot = lax.rem(n, 2)

    def matmul():
        acc_ref[slot] = jnp.dot(h_ref[...], w_ref[...], preferred_element_type=F32)

    def epilogue(kinds):
        prev = acc_ref.at[1 - slot]
        for c, kind in enumerate(kinds):
            sl = slice(c * LANES, (c + 1) * LANES)
            o_ref[:, sl] = _proj_epilogue(kind, prev[:, sl], tab_ref, qkg_ref).astype(o_ref.dtype)

    pl.when(n == 0)(matmul)
    for tiles, kinds in groups:
        prev_in_group = functools.reduce(jnp.logical_or, [n - 1 == t for t in tiles])

        @pl.when(prev_in_group & (n < n_tiles))
        def _(kinds=kinds):
            epilogue(kinds)
            matmul()

        if n_tiles - 1 in tiles:
            pl.when(n == n_tiles)(functools.partial(epilogue, kinds))


def _proj_layout(d):
    a_heads, b_heads, c_heads = d // 256, d // 512, d // 256
    c_kv = c_heads // C_GROUP
    segs = [("aq", a_heads, ("rope64", A_DIM ** -0.5 * LOG2E)), ("ak", a_heads, ("rope64", 1.0)),
            ("av", a_heads, ("none",)),
            ("bq", b_heads, ("rope128", 1.0)), ("bk", b_heads, ("rope128", B_QK ** -0.5)),
            ("bv", 2 * b_heads, ("none",)), ("bg", 2 * b_heads, ("silu",)),
            ("cq", c_heads, ("norm_rope128", C_DIM ** -0.5 * LOG2E, 0)), ("ck", c_kv, ("norm_rope128", 1.0, 1)),
            ("cv", c_kv, ("none",))]
    kinds, off = [], {}
    for name, nblk, kind in segs:
        off[name] = len(kinds)
        kinds += [kind] * nblk
    return kinds, off


def _proj(h, w, tabs, qkg, kinds):
    n_batch, rows, d = h.shape
    width = w.shape[1]
    tm = _pick(rows, 1152, 16)
    tn = _pick(width, 512, LANES)
    per = tn // LANES
    by_kinds = {}
    for t in range(width // tn):
        by_kinds.setdefault(tuple(kinds[t * per:(t + 1) * per]), []).append(t)
    groups = tuple((tuple(tiles), ks) for ks, tiles in by_kinds.items())
    n_tiles = width // tn
    return pl.pallas_call(
        functools.partial(_proj_kernel, groups=groups, n_tiles=n_tiles),
        grid=(n_batch, rows // tm, n_tiles + 1),
        in_specs=[pl.BlockSpec((None, tm, d), lambda b, j, n: (b, j, 0)),
                  pl.BlockSpec((d, tn), lambda b, j, n: (0, jnp.minimum(n, n_tiles - 1))),
                  pl.BlockSpec((5, tm, LANES), lambda b, j, n: (0, j, 0)),
                  pl.BlockSpec((2, LANES), lambda b, j, n: (0, 0))],
        out_specs=pl.BlockSpec((None, tm, tn), lambda b, j, n: (b, j, jnp.maximum(n - 1, 0))),
        out_shape=jax.ShapeDtypeStruct((n_batch, rows, width), BF16),
        scratch_shapes=[pltpu.VMEM((2, tm, tn), F32)],
        compiler_params=_params(3),
    )(h, w, tabs, qkg)


def _scores(q, k):
    return lax.dot_general(q, k, (((1,), (1,)), ((), ())), preferred_element_type=F32)


def _softmax_pv_rowmax(q, k, vx):
    s = _scores(q, k)
    e = jnp.exp2(s - jnp.max(s, axis=-1, keepdims=True)).astype(BF16)
    r = jnp.dot(e, vx, preferred_element_type=F32)
    return r[:, :LANES] / r[:, LANES:LANES + 1]


def _bound_shift(q, ksq_max):
    qf = q.astype(F32)
    return BOUND_SLACK * jnp.sqrt(jnp.sum(qf * qf, axis=-1, keepdims=True) * ksq_max)


def _attn_kernel(q_ref, k_ref, v_ref, lam_ref, g_ref, o_ref, vx_ref, ksq_ref, *,
                 diff, lam_init, n_lat_tiles, n_lat, has_ctx, heads, kv_shared):
    i = pl.program_id(2)
    n_kv = 1 if kv_shared else heads
    n_keys = vx_ref.shape[1]

    def lanes(j):
        return slice(j * LANES, (j + 1) * LANES)

    @pl.when(i == 0)
    def _():
        for j in range(n_kv):
            vx_ref[j, :, :LANES] = v_ref[:, lanes(j)]
            vx_ref[j, :, LANES:] = jnp.ones((n_keys, LANES), BF16)
            k = k_ref[:, lanes(j)]
            ksq = jnp.dot(k * k, jnp.ones((LANES, LANES), BF16), preferred_element_type=F32)
            ksq_ref[j] = jnp.broadcast_to(jnp.max(ksq, axis=0, keepdims=True), (SUBLANES, LANES))

    def run(key_rows, nq):
        def operands(hh):
            j = 0 if kv_shared else hh
            q = q_ref[:nq, lanes(hh)]
            if diff:
                qf = q.astype(F32)
                lane = lax.broadcasted_iota(jnp.int32, qf.shape, 1)
                qs = (jnp.where(lane < A_DIM, qf, 0.0).astype(BF16), jnp.where(lane >= A_DIM, qf, 0.0).astype(BF16))
            else:
                qs = (q,)
            return qs, k_ref[key_rows, lanes(j)], vx_ref[j, key_rows, :], ksq_ref[j, 0:1, 0:1]

        def finish(hh, outs):
            if diff:
                lv = lam_ref[...]
                lam = (jnp.exp(jnp.sum(lv[0:1] * lv[1:2], axis=-1, keepdims=True))
                       - jnp.exp(jnp.sum(lv[2:3] * lv[3:4], axis=-1, keepdims=True)) + lam_init)
                o = _rms(outs[0] - lam * outs[1], EPS) * g_ref[...] * (1.0 - lam_init)
            else:
                o = outs[0]
            o_ref[:nq, lanes(hh)] = o.astype(o_ref.dtype)

        ops = [(hh, qq, k, vx, ksq) for hh in range(heads) for qs, k, vx, ksq in [operands(hh)] for qq in qs]
        shifted = [_scores(qq, k) - _bound_shift(qq, ksq) for _, qq, k, _, ksq in ops]
        exps = [jnp.exp2(s).astype(BF16) for s in shifted]
        prods = [jnp.dot(e, op[3], preferred_element_type=F32) for e, op in zip(exps, ops)]
        parts = [[] for _ in range(heads)]
        for op, r in zip(ops, prods):
            parts[op[0]].append((r[:, :LANES], r[:, LANES:LANES + 1]))
        den_min = functools.reduce(jnp.minimum, [den for head in parts for _, den in head])
        precise = jnp.min(den_min) >= DEN_FLOOR

        @pl.when(precise)
        def _():
            for hh in range(heads):
                finish(hh, [num / den for num, den in parts[hh]])

        @pl.when(jnp.logical_not(precise))
        def _():
            for hh in range(heads):
                qs, k, vx, _ = operands(hh)
                finish(hh, [_softmax_pv_rowmax(qq, k, vx) for qq in qs])

    tq = q_ref.shape[0]
    if not has_ctx:
        run(slice(None), tq)
        return

    @pl.when(i < n_lat_tiles)
    def _():
        run(slice(None), tq)

    @pl.when(i >= n_lat_tiles)
    def _():
        run(slice(n_lat, None), min(tq, n_keys - n_lat))


def _query_tile(n_lat, n_ctx, has_ctx):
    tq = _pick(n_lat, 512, 16)
    assert not has_ctx or n_ctx % tq == 0 or n_ctx < tq
    return tq


def _attention(p, lam_vec, subln_g, rows, n_lat, n_ctx, n_heads, q_off, k_off, v_off, kv_group, diff, lam_init):
    n_batch, n_keys, _ = p.shape
    has_ctx = rows > n_lat
    tq = _query_tile(n_lat, n_ctx, has_ctx)
    kv_shared = kv_group > 1
    heads = kv_group if kv_shared else 2
    assert n_heads % heads == 0 and q_off % heads == 0 and (kv_shared or (k_off % heads == 0 and v_off % heads == 0))
    kv_w = LANES if kv_shared else heads * LANES
    kv_blk = (lambda off: (lambda b, h, i: (b, 0, off + h))) if kv_shared else \
             (lambda off: (lambda b, h, i: (b, 0, off // heads + h)))
    kern = functools.partial(_attn_kernel, diff=diff, lam_init=lam_init, n_lat_tiles=n_lat // tq,
                             n_lat=n_lat, has_ctx=has_ctx, heads=heads, kv_shared=kv_shared)
    n_kv = 1 if kv_shared else heads
    return pl.pallas_call(
        kern,
        grid=(n_batch, n_heads // heads, pl.cdiv(rows, tq)),
        in_specs=[pl.BlockSpec((None, tq, heads * LANES), lambda b, h, i: (b, i, q_off // heads + h)),
                  pl.BlockSpec((None, n_keys, kv_w), kv_blk(k_off)),
                  pl.BlockSpec((None, n_keys, kv_w), kv_blk(v_off)),
                  pl.BlockSpec(lam_vec.shape, lambda b, h, i: (0, 0)),
                  pl.BlockSpec((1, LANES), lambda b, h, i: (0, 0))],
        out_specs=pl.BlockSpec((None, tq, heads * LANES), lambda b, h, i: (b, i, h)),
        out_shape=jax.ShapeDtypeStruct((n_batch, rows, n_heads * LANES), BF16),
        scratch_shapes=[pltpu.VMEM((n_kv, n_keys, 2 * LANES), BF16), pltpu.VMEM((n_kv, SUBLANES, LANES), F32)],
        compiler_params=_params(3),
    )(p, p, p, lam_vec, subln_g.reshape(1, LANES))


def _ret_decay(lgf, lgb, t, pf, pb):
    df = (t - pf).astype(F32)
    db = (pb - t).astype(F32)
    fwd = jnp.where(df >= 0, jnp.exp(lgf * jnp.maximum(df, 0.0)), 0.0)
    bwd = jnp.where(db >= 0, jnp.exp(lgb * jnp.maximum(db, 0.0)), 0.0)
    return fwd + bwd


def _ret_kernel(ld_ref, q_ref, k_ref, v_ref, g_ref, o_ref, d_ref, *, tq, n_keys, n_lat, n_heads, has_ctx, heads):
    hp, i, b = pl.program_id(0), pl.program_id(1), pl.program_id(2)
    n_ctx = n_keys - n_lat
    n_lat_tiles = n_lat // tq
    t_loc = lax.broadcasted_iota(jnp.int32, (tq, 1), 0)

    def log_decays(j):
        return ld_ref[hp * heads + j], ld_ref[n_heads + hp * heads + j]

    def lat_decay():
        col = lax.broadcasted_iota(jnp.int32, (1, n_keys), 1)
        pf = jnp.where(col < n_lat, col, col - n_keys)
        for j in range(heads):
            d_ref[j] = _ret_decay(*log_decays(j), i * tq + t_loc, pf, col)

    def ctx_decay():
        col = lax.broadcasted_iota(jnp.int32, (1, n_ctx), 1)
        for j in range(heads):
            d_ref[j, :, :n_ctx] = _ret_decay(*log_decays(j), (i - n_lat_tiles) * tq + t_loc, col, col)

    def run(key_rows, n_cols, nq):
        def qk(j):
            sl = slice(j * B_QK, (j + 1) * B_QK)
            return lax.dot_general(q_ref[:nq, sl], k_ref[key_rows, sl], (((1,), (1,)), ((), ())),
                                   preferred_element_type=F32)

        scores = [qk(j) for j in range(heads)]
        weighted = [(s * d_ref[j, :nq, :n_cols]).astype(BF16) for j, s in enumerate(scores)]
        for j, w in enumerate(weighted):
            sl = slice(j * B_V, (j + 1) * B_V)
            o = jnp.dot(w, v_ref[key_rows, sl], preferred_element_type=F32)
            mu = jnp.mean(o, axis=-1, keepdims=True)
            var = jnp.mean(jnp.square(o - mu), axis=-1, keepdims=True)
            o_ref[:nq, sl] = ((o - mu) * lax.rsqrt(var + GN_EPS) * g_ref[:nq, sl].astype(F32)).astype(o_ref.dtype)

    if not has_ctx:
        pl.when(b == 0)(lat_decay)
        run(slice(None), n_keys, tq)
        return

    pl.when((b == 0) & (i < n_lat_tiles))(lat_decay)
    pl.when((b == 0) & (i >= n_lat_tiles))(ctx_decay)

    @pl.when(i < n_lat_tiles)
    def _():
        run(slice(None), n_keys, tq)

    @pl.when(i >= n_lat_tiles)
    def _():
        run(slice(n_lat, None), n_ctx, min(tq, n_ctx))


def _retention(p, log_decay, rows, n_lat, n_ctx, n_heads, off):
    n_batch, n_keys, _ = p.shape
    has_ctx = rows > n_lat
    tq = _query_tile(n_lat, n_ctx, has_ctx)
    heads = 2 if n_heads % 2 == 0 else 1
    v_off, g_off = off["bv"] // 2, off["bg"] // 2
    assert all(o % heads == 0 for o in (off["bq"], off["bk"], v_off, g_off))
    kern = functools.partial(_ret_kernel, tq=tq, n_keys=n_keys, n_lat=n_lat, n_heads=n_heads, has_ctx=has_ctx,
                             heads=heads)
    grid_spec = pltpu.PrefetchScalarGridSpec(
        num_scalar_prefetch=1,
        grid=(n_heads // heads, pl.cdiv(rows, tq), n_batch),
        in_specs=[pl.BlockSpec((None, tq, heads * B_QK), lambda h, i, b, ld: (b, i, off["bq"] // heads + h)),
                  pl.BlockSpec((None, n_keys, heads * B_QK), lambda h, i, b, ld: (b, 0, off["bk"] // heads + h)),
                  pl.BlockSpec((None, n_keys, heads * B_V), lambda h, i, b, ld: (b, 0, v_off // heads + h)),
                  pl.BlockSpec((None, tq, heads * B_V), lambda h, i, b, ld: (b, i, g_off // heads + h))],
        out_specs=pl.BlockSpec((None, tq, heads * B_V), lambda h, i, b, ld: (b, i, h)),
        scratch_shapes=[pltpu.VMEM((heads, tq, n_keys), F32)])
    return pl.pallas_call(
        kern, grid_spec=grid_spec,
        out_shape=jax.ShapeDtypeStruct((n_batch, rows, n_heads * B_V), BF16),
        compiler_params=_params(3),
    )(log_decay.reshape(-1).astype(F32), p, p, p, p)


def _merge_kernel(h_ref, oa_ref, ob_ref, oc_ref, ga_ref, gb_ref, gc_ref, wb_ref, y_ref):
    h = h_ref[...]
    y = None
    for idx, (o_ref, wg_ref) in enumerate(((oa_ref, ga_ref), (ob_ref, gb_ref), (oc_ref, gc_ref))):
        gate = jax.nn.sigmoid(jnp.dot(h, wg_ref[...], preferred_element_type=F32))
        term = gate * jnp.dot(o_ref[...], wb_ref[idx], preferred_element_type=F32)
        y = term if y is None else y + term
    y_ref[...] = y.astype(y_ref.dtype)


def _merge(h, outs, w_gate, w_branch, rows):
    n_batch, _, d = h.shape
    bw = w_branch.shape[1]
    tm = _pick(rows, 1152, 16)
    tn = _pick(d, 512, LANES)
    nn = d // tn
    o_spec = pl.BlockSpec((None, tm, bw), lambda b, j, n: (b, j, 0))
    g_specs = [pl.BlockSpec((d, tn), functools.partial(lambda b, j, n, k: (0, k * nn + n), k=k)) for k in range(3)]
    return pl.pallas_call(
        _merge_kernel,
        grid=(n_batch, rows // tm, nn),
        in_specs=[pl.BlockSpec((None, tm, d), lambda b, j, n: (b, j, 0)), o_spec, o_spec, o_spec,
                  *g_specs, pl.BlockSpec((3, bw, tn), lambda b, j, n: (0, 0, n))],
        out_specs=pl.BlockSpec((None, tm, tn), lambda b, j, n: (b, j, n)),
        out_shape=jax.ShapeDtypeStruct((n_batch, rows, d), BF16),
        compiler_params=_params(3),
    )(h, *outs, w_gate, w_gate, w_gate, w_branch)


def _out_resid_kernel(y_ref, w_ref, x_ref, g_ref, gn_ref, ml_ref, mc_ref, o_ref, hn_ref, *, tm, n_lat, has_ctx):
    is_ctx = _ctx_rows(pl.program_id(1), tm, n_lat, has_ctx)
    f = jnp.dot(y_ref[...], w_ref[...], preferred_element_type=F32)
    x_new = _gated_residual(x_ref[...], f, g_ref, ml_ref, mc_ref, K_GATE_TOK, is_ctx)
    o_ref[...] = x_new
    hn_ref[...] = _modulated_norm(x_new, gn_ref, ml_ref, mc_ref, K_SHIFT_CH, K_SCALE_CH, is_ctx).astype(hn_ref.dtype)


def _out_resid(y, w_out, x, g, g_next, mods, rows, n_lat, next_dtype):
    n_batch, _, d = y.shape
    tm = _pick(rows, 512, 16)
    kern = functools.partial(_out_resid_kernel, tm=tm, n_lat=n_lat, has_ctx=rows > n_lat)
    row_spec = pl.BlockSpec((None, tm, d), lambda b, j: (b, j, 0))
    vec_spec = pl.BlockSpec((1, d), lambda b, j: (0, 0))
    return pl.pallas_call(
        kern,
        grid=(n_batch, rows // tm),
        in_specs=[row_spec, pl.BlockSpec((d, d), lambda b, j: (0, 0)), row_spec, vec_spec, vec_spec,
                  pl.BlockSpec((None, 6, 1, d), lambda b, j: (b, 0, 0, 0)),
                  pl.BlockSpec((None, 6, 1, d), lambda b, j: (n_batch, 0, 0, 0))],
        out_specs=[row_spec, row_spec],
        out_shape=[jax.ShapeDtypeStruct((n_batch, rows, d), F32), jax.ShapeDtypeStruct((n_batch, rows, d), next_dtype)],
        compiler_params=_params(2),
    )(y, w_out, x, g.reshape(1, d), g_next.reshape(1, d), mods, mods)


def _ffn_kernel(*refs, tm, n_lat, has_ctx, has_next):
    if has_next:
        (h_ref, wg_ref, wu_ref, wo_ref, x_ref, g_ref, ml_ref, mc_ref, gn_ref, mln_ref, mcn_ref,
         o_ref, hn_ref) = refs
    else:
        h_ref, wg_ref, wu_ref, wo_ref, x_ref, g_ref, ml_ref, mc_ref, o_ref = refs
    f = pl.program_id(2)

    @pl.when(f == 0)
    def _():
        o_ref[...] = jnp.zeros_like(o_ref)

    h = h_ref[...]
    gate = jnp.dot(h, wg_ref[...], preferred_element_type=F32)
    up = jnp.dot(h, wu_ref[...], preferred_element_type=F32)
    o_ref[...] += jnp.dot((_silu(gate) * up).astype(BF16), wo_ref[...], preferred_element_type=F32)

    @pl.when(f == pl.num_programs(2) - 1)
    def _():
        is_ctx = _ctx_rows(pl.program_id(1), tm, n_lat, has_ctx)
        x_new = _gated_residual(x_ref[...], o_ref[...], g_ref, ml_ref, mc_ref, K_GATE_CH, is_ctx)
        o_ref[...] = x_new
        if has_next:
            hn_ref[...] = _modulated_norm(x_new, gn_ref, mln_ref, mcn_ref, K_SHIFT_TOK, K_SCALE_TOK,
                                          is_ctx).astype(hn_ref.dtype)


def _ffn_dense(h, w_in, w_out, x, g, mods, rows, n_lat, g_next, mods_next):
    n_batch, _, d = h.shape
    ff = w_out.shape[0]
    tm = _pick(rows, 576, 16)
    tf = _pick(ff, 512, LANES)
    nf = ff // tf
    has_next = g_next is not None
    kern = functools.partial(_ffn_kernel, tm=tm, n_lat=n_lat, has_ctx=rows > n_lat, has_next=has_next)
    row_spec = pl.BlockSpec((None, tm, d), lambda b, j, f: (b, j, 0))
    vec_spec = pl.BlockSpec((1, d), lambda b, j, f: (0, 0))
    mod_specs = [pl.BlockSpec((None, 6, 1, d), lambda b, j, f: (b, 0, 0, 0)),
                 pl.BlockSpec((None, 6, 1, d), lambda b, j, f: (n_batch, 0, 0, 0))]
    in_specs = [row_spec,
                pl.BlockSpec((d, tf), lambda b, j, f: (0, f)),
                pl.BlockSpec((d, tf), lambda b, j, f: (0, nf + f)),
                pl.BlockSpec((tf, d), lambda b, j, f: (f, 0)),
                row_spec, vec_spec, *mod_specs]
    args = [h, w_in, w_in, w_out, x, g.reshape(1, d), mods, mods]
    out_specs, out_shape = [row_spec], [jax.ShapeDtypeStruct((n_batch, rows, d), F32)]
    if has_next:
        in_specs += [vec_spec, *mod_specs]
        args += [g_next.reshape(1, d), mods_next, mods_next]
        out_specs.append(row_spec)
        out_shape.append(jax.ShapeDtypeStruct((n_batch, rows, d), BF16))
    outs = pl.pallas_call(
        kern,
        grid=(n_batch, rows // tm, nf),
        in_specs=in_specs, out_specs=out_specs, out_shape=out_shape,
        compiler_params=_params(3),
    )(*args)
    return (outs[0], outs[1]) if has_next else (outs[0], None)


def _router_kernel(h_ref, w_ref, idx_ref, wt_ref, *, n_experts):
    logits = jnp.dot(h_ref[...].astype(BF16), w_ref[...], preferred_element_type=F32)
    lane = lax.broadcasted_iota(jnp.int32, logits.shape, 1)
    neg = jnp.float32(-jnp.inf)
    lg = jnp.where(lane < n_experts, logits, neg)
    m1 = jnp.max(lg, axis=-1, keepdims=True)
    i1 = jnp.min(jnp.where(lg == m1, lane, LANES), axis=-1, keepdims=True)
    lg2 = jnp.where(lane == i1, neg, lg)
    m2 = jnp.max(lg2, axis=-1, keepdims=True)
    i2 = jnp.min(jnp.where(lg2 == m2, lane, LANES), axis=-1, keepdims=True)
    e2 = jnp.exp(m2 - m1)
    den = 1.0 + e2
    idx_ref[...] = jnp.where(lane == 0, i1, jnp.where(lane == 1, i2, 0))
    wt_ref[...] = jnp.where(lane == 0, 1.0 / den, jnp.where(lane == 1, e2 / den, 0.0))


def _router(h, w_router):
    n_tok, d = h.shape
    n_experts = w_router.shape[1]
    tm = _pick(n_tok, 1024, 8)
    w_pad = jnp.zeros((d, LANES), BF16).at[:, :n_experts].set(w_router.astype(BF16))
    out_spec = pl.BlockSpec((tm, LANES), lambda i: (i, 0))
    return pl.pallas_call(
        functools.partial(_router_kernel, n_experts=n_experts),
        grid=(n_tok // tm,),
        in_specs=[pl.BlockSpec((tm, d), lambda i: (i, 0)), pl.BlockSpec((d, LANES), lambda i: (0, 0))],
        out_specs=[out_spec, out_spec],
        out_shape=[jax.ShapeDtypeStruct((n_tok, LANES), jnp.int32), jax.ShapeDtypeStruct((n_tok, LANES), F32)],
        compiler_params=_params(1),
    )(h, w_pad)


def _route_plan(top_idx, n_experts, tm, n_tiles):
    n_tok = top_idx.shape[0]
    e_flat = top_idx.reshape(-1)
    onehot = (e_flat[:, None] == jnp.arange(n_experts, dtype=jnp.int32)[None, :]).astype(jnp.int32)
    csum = jnp.cumsum(onehot, axis=0)
    rank = jnp.sum((csum - onehot) * onehot, axis=1)
    tiles_e = (csum[-1] + tm - 1) // tm
    tiles_end = jnp.cumsum(tiles_e)
    pos = (tiles_end - tiles_e)[e_flat] * tm + rank
    n_used = tiles_end[-1]
    tile_ids = jnp.arange(n_tiles, dtype=jnp.int32)
    tile_expert = jnp.sum((tile_ids[:, None] >= tiles_end[None, :]).astype(jnp.int32), axis=1)
    tile_expert = jnp.minimum(tile_expert, n_experts - 1)
    last_expert = tile_expert[jnp.maximum(n_used - 1, 0)]
    tile_expert = jnp.where(tile_ids < n_used, tile_expert, last_expert)
    row_token = jnp.zeros((n_tiles * tm,), jnp.int32).at[pos].set(jnp.arange(TOP_K * n_tok, dtype=jnp.int32) // TOP_K)
    return row_token, pos.reshape(n_tok, TOP_K).astype(jnp.int32), tile_expert, n_used.reshape(1).astype(jnp.int32)


def _moe_ffn_kernel(te_ref, nv_ref, tok0_ref, tokn_ref, src_ref, wg_ref, wu_ref, wo_ref, o_ref,
                    land_ref, xb_ref, sem, *, tm, nf, n_tiles):
    i, f = pl.program_id(0), pl.program_id(1)
    n_used = nv_ref[0]
    valid = i < n_used
    n_started = land_ref.shape[0]
    chunk = n_started // nf

    def row_copy(tok_ref, r):
        tok = tok_ref[0, jnp.minimum(r, tm - 1)]
        return pltpu.make_async_copy(src_ref.at[pl.ds(tok, 1)], land_ref.at[pl.ds(r, 1)], sem)

    def wait_tile():
        pltpu.make_async_copy(src_ref.at[pl.ds(0, n_started)], land_ref, sem).wait()

    @pl.when((i == 0) & (f == 0))
    def _():
        def start(r, carry):
            row_copy(tok0_ref, r).start()
            return carry
        lax.fori_loop(0, n_started, start, 0, unroll=8)

    @pl.when((f == 0) & (i <= n_used))
    def _():
        wait_tile()

    @pl.when(f == 0)
    def _():
        o_ref[...] = jnp.zeros_like(o_ref)

    @pl.when(valid & (f == 0))
    def _():
        xb_ref[...] = land_ref[:tm, :].astype(BF16)

    @pl.when(valid)
    def _():
        for u in range(chunk):
            row_copy(tokn_ref, f * chunk + u).start()
        xb = xb_ref[...]
        gate = jnp.dot(xb, wg_ref[...], preferred_element_type=F32)
        up = jnp.dot(xb, wu_ref[...], preferred_element_type=F32)
        o_ref[...] += jnp.dot((_silu(gate) * up).astype(BF16), wo_ref[...], preferred_element_type=F32)

    @pl.when(valid & (i == n_tiles - 1) & (f == nf - 1))
    def _():
        wait_tile()


def _moe_ffn(src, row_token, w_in, w_out, tile_expert, n_used, tm):
    n_rows = row_token.shape[0]
    d = src.shape[1]
    ff = w_out.shape[1]
    tf = _pick(ff, 512, LANES)
    nf = ff // tf
    n_tiles = n_rows // tm
    chunk = -(-pl.cdiv(tm, nf) // SUBLANES) * SUBLANES

    def f_eff(i, f, nv):
        return jnp.where(i < nv[0], f, nf - 1)

    tok_blk = (None, 1, tm)
    grid_spec = pltpu.PrefetchScalarGridSpec(
        num_scalar_prefetch=2,
        grid=(n_tiles, nf),
        in_specs=[pl.BlockSpec(tok_blk, lambda i, f, te, nv: (0, 0, 0), memory_space=pltpu.SMEM),
                  pl.BlockSpec(tok_blk, lambda i, f, te, nv: (jnp.minimum(i + 1, n_tiles - 1), 0, 0),
                               memory_space=pltpu.SMEM),
                  pl.BlockSpec(memory_space=pl.ANY),
                  pl.BlockSpec((None, d, tf), lambda i, f, te, nv: (te[i], 0, f_eff(i, f, nv))),
                  pl.BlockSpec((None, d, tf), lambda i, f, te, nv: (te[i], 0, nf + f_eff(i, f, nv))),
                  pl.BlockSpec((None, tf, d), lambda i, f, te, nv: (te[i], f_eff(i, f, nv), 0))],
        out_specs=pl.BlockSpec((tm, d), lambda i, f, te, nv: (i, 0)),
        scratch_shapes=[pltpu.VMEM((nf * chunk, d), src.dtype), pltpu.VMEM((tm, d), BF16),
                        pltpu.SemaphoreType.DMA(())])
    tok = row_token.reshape(n_tiles, 1, tm)
    return pl.pallas_call(
        functools.partial(_moe_ffn_kernel, tm=tm, nf=nf, n_tiles=n_tiles), grid_spec=grid_spec,
        out_shape=jax.ShapeDtypeStruct((n_rows, d), F32),
        compiler_params=_params(2),
    )(tile_expert, n_used, tok, tok, src, w_in, w_in, w_out)


def _combine_kernel(pos_ref, posn_ref, ys_ref, wt_ref, x_ref, g_ref, ml_ref, mc_ref, o_ref, buf_ref, sems, *,
                    tc, n_lat, has_ctx):
    step = pl.program_id(0) * pl.num_programs(1) + pl.program_id(1)
    n_steps = pl.num_programs(0) * pl.num_programs(1)
    slot = lax.rem(step, 2)

    def start_rows(p_ref, sl):
        def body(r, carry):
            for k in range(TOP_K):
                pltpu.make_async_copy(ys_ref.at[pl.ds(p_ref[0, TOP_K * r + k], 1)],
                                      buf_ref.at[sl, k, pl.ds(r, 1)], sems.at[sl]).start()
            return carry
        lax.fori_loop(0, tc, body, 0, unroll=4)

    pl.when(step == 0)(lambda: start_rows(pos_ref, slot))
    pl.when(step + 1 < n_steps)(lambda: start_rows(posn_ref, 1 - slot))
    for k in range(TOP_K):
        pltpu.make_async_copy(ys_ref.at[pl.ds(0, tc)], buf_ref.at[slot, k], sems.at[slot]).wait()
    wt = wt_ref[...]
    f = wt[:, 0:1] * buf_ref[slot, 0]
    for k in range(1, TOP_K):
        f = f + wt[:, k:k + 1] * buf_ref[slot, k]
    is_ctx = _ctx_rows(pl.program_id(1), tc, n_lat, has_ctx)
    o_ref[...] = _gated_residual(x_ref[...], f, g_ref, ml_ref, mc_ref, K_GATE_CH, is_ctx)


def _combine(ys, pos, wts, x, g, mods, rows, n_lat):
    n_batch, _, d = x.shape
    tc = _pick(rows, 256, 8)
    nj = rows // tc
    n_steps = n_batch * nj
    kern = functools.partial(_combine_kernel, tc=tc, n_lat=n_lat, has_ctx=rows > n_lat)
    row_spec = pl.BlockSpec((None, tc, d), lambda b, j: (b, j, 0))
    pos_blk = (None, 1, TOP_K * tc)
    pos_steps = pos.reshape(n_steps, 1, TOP_K * tc)
    return pl.pallas_call(
        kern,
        grid=(n_batch, nj),
        in_specs=[pl.BlockSpec(pos_blk, lambda b, j: (b * nj + j, 0, 0), memory_space=pltpu.SMEM),
                  pl.BlockSpec(pos_blk, lambda b, j: (jnp.minimum(b * nj + j + 1, n_steps - 1), 0, 0),
                               memory_space=pltpu.SMEM),
                  pl.BlockSpec(memory_space=pl.ANY),
                  pl.BlockSpec((tc, LANES), lambda b, j: (b * nj + j, 0)),
                  row_spec,
                  pl.BlockSpec((1, d), lambda b, j: (0, 0)),
                  pl.BlockSpec((None, 6, 1, d), lambda b, j: (b, 0, 0, 0)),
                  pl.BlockSpec((None, 6, 1, d), lambda b, j: (n_batch, 0, 0, 0))],
        out_specs=row_spec,
        out_shape=jax.ShapeDtypeStruct((n_batch, rows, d), F32),
        scratch_shapes=[pltpu.VMEM((2, TOP_K, tc, d), F32), pltpu.SemaphoreType.DMA((2,))],
        compiler_params=_params(2),
    )(pos_steps, pos_steps, ys, wts, x, g.reshape(1, d), mods, mods)


def _ffn_moe(h, w_router, w_in, w_out, x, g, mods, rows, n_lat):
    n_batch, _, d = h.shape
    n_tok = n_batch * rows
    n_experts = w_router.shape[1]
    tm = min(768, max(16, (TOP_K * n_tok // n_experts) // 16 * 16))
    n_tiles = (TOP_K * n_tok + n_experts * (tm - 1)) // tm
    h_flat = h.reshape(n_tok, d)
    idx, wts = _router(h_flat, w_router)
    row_token, pos, tile_expert, n_used = _route_plan(idx[:, :TOP_K], n_experts, tm, n_tiles)
    ys = _moe_ffn(h_flat, row_token, w_in, w_out, tile_expert, n_used, tm)
    return _combine(ys, pos, wts, x, g, mods, rows, n_lat)


def kernel(x, c, ctx, c_ctx, w_mod, b_mod, norm_g, w_in, qk_norm_g, diff_lambda, diff_subln_g, ret_log_decay,
           w_branch, w_branch_gate, w_out, ffn_w_in, ffn_w_out, moe_router, moe_w_in, moe_w_out):
    n_batch, n_lat, d = x.shape
    n_ctx = ctx.shape[1]
    depth = w_mod.shape[0]
    a_heads, b_heads, c_heads = d // 256, d // 512, d // 256
    kinds, off = _proj_layout(d)

    mod_rows = (n_batch + 1 + 7) // 8 * 8
    cond = jnp.zeros((mod_rows, d), F32).at[:n_batch].set(c).at[n_batch].set(c_ctx)
    mods_all = _modvec(cond, w_mod, b_mod).reshape(depth, mod_rows, 6, 1, d)
    tabs = _rope_tables(n_lat, n_ctx)

    xs = jnp.concatenate([x, ctx], axis=1)
    n_all = n_lat + n_ctx
    h = _norm_mod(xs, norm_g[0, 0], mods_all[0], n_all, n_lat, K_SHIFT_TOK, K_SCALE_TOK, BF16)
    for i in range(depth):
        last = i == depth - 1
        rows = n_lat if last else n_all
        lam_init = 0.8 - 0.6 * math.exp(-0.3 * i)
        mods = mods_all[i]
        dense = i % 2 == 0
        j = i // 2
        p = _proj(h, w_in[i].astype(BF16), tabs, qk_norm_g[i], kinds)
        o_a = _attention(p, diff_lambda[i], diff_subln_g[i], rows, n_lat, n_ctx, a_heads,
                         off["aq"], off["ak"], off["av"], 1, True, lam_init)
        o_b = _retention(p, ret_log_decay[i], rows, n_lat, n_ctx, b_heads, off)
        o_c = _attention(p, diff_lambda[i], diff_subln_g[i], rows, n_lat, n_ctx, c_heads,
                         off["cq"], off["ck"], off["cv"], C_GROUP, False, lam_init)
        y = _merge(h, (o_a, o_b, o_c), w_branch_gate[i].astype(BF16), w_branch[i].astype(BF16), rows)
        xs, h2 = _out_resid(y, w_out[i].astype(BF16), xs, norm_g[i, 1], norm_g[i, 2], mods, rows, n_lat,
                            BF16 if dense else F32)
        if dense:
            g_next, mods_next = (None, None) if last else (norm_g[i + 1, 0], mods_all[i + 1])
            xs, h = _ffn_dense(h2, ffn_w_in[j].astype(BF16), ffn_w_out[j].astype(BF16), xs, norm_g[i, 3], mods,
                               rows, n_lat, g_next, mods_next)
        else:
            xs = _ffn_moe(h2, moe_router[j], moe_w_in[j].astype(BF16), moe_w_out[j].astype(BF16), xs,
                          norm_g[i, 3], mods, rows, n_lat)
            if not last:
                h = _norm_mod(xs, norm_g[i + 1, 0], mods_all[i + 1], n_all, n_lat, K_SHIFT_TOK, K_SCALE_TOK, BF16)
    return xs[:, :n_lat]
```

```python
import functools
import math

import jax
import jax.numpy as jnp
from jax import lax
from jax.experimental import pallas as pl
from jax.experimental.pallas import tpu as pltpu

F32 = jnp.float32
BF16 = jnp.bfloat16

GRID_W = 64
ROPE_THETA = 10000.0
EPS = 1e-6
GN_EPS = 1e-5
A_DIM = 64
B_QK = 128
B_V = 256
C_DIM = 128
C_GROUP = 4
TOP_K = 2
K_SHIFT_TOK, K_SCALE_TOK, K_GATE_TOK, K_SHIFT_CH, K_SCALE_CH, K_GATE_CH = range(6)
LANES = 128
SUBLANES = 8
LOG2E = 1.4426950408889634
BOUND_SLACK = 1.02
DEN_FLOOR = 2.0 ** -88
VMEM_LIMIT_BYTES = 56 * 1024 * 1024


def _pick(n, target, mult):
    best = None
    for t in range(mult, min(n, target) + 1, mult):
        if n % t == 0:
            best = t
    return best if best is not None else n


def _params(n_axes):
    return pltpu.CompilerParams(dimension_semantics=("arbitrary",) * n_axes,
                                vmem_limit_bytes=VMEM_LIMIT_BYTES)


def _silu(v):
    return v * jax.nn.sigmoid(v)


def _rms(v, eps):
    return v * lax.rsqrt(jnp.mean(v * v, axis=-1, keepdims=True) + eps)


def _ctx_rows(tile_idx, tm, n_lat, has_ctx):
    if not has_ctx:
        return None
    row = tile_idx * tm + lax.broadcasted_iota(jnp.int32, (tm, 1), 0)
    return row >= n_lat


def _mod_vec(ml_ref, mc_ref, k, is_ctx):
    return ml_ref[k] if is_ctx is None else jnp.where(is_ctx, mc_ref[k], ml_ref[k])


def _modulated_norm(x, g_ref, ml_ref, mc_ref, k_shift, k_scale, is_ctx):
    y = _rms(x, EPS) * g_ref[...]
    return y * (1.0 + _mod_vec(ml_ref, mc_ref, k_scale, is_ctx)) + _mod_vec(ml_ref, mc_ref, k_shift, is_ctx)


def _gated_residual(x, f, g_ref, ml_ref, mc_ref, k_gate, is_ctx):
    return x + _mod_vec(ml_ref, mc_ref, k_gate, is_ctx) * (_rms(f, EPS) * g_ref[...])


def _modvec_kernel(c_ref, w_ref, b_ref, o_ref):
    a = _silu(c_ref[...]).astype(BF16)
    o_ref[...] = jnp.dot(a, w_ref[...].astype(BF16), preferred_element_type=F32) + b_ref[...]


def _modvec(cond, w_mod, b_mod):
    depth, d, n6 = w_mod.shape
    rows = cond.shape[0]
    tn = _pick(n6, 1024, LANES)
    return pl.pallas_call(
        _modvec_kernel,
        grid=(depth, n6 // tn),
        in_specs=[pl.BlockSpec((rows, d), lambda i, n: (0, 0)),
                  pl.BlockSpec((None, d, tn), lambda i, n: (i, 0, n)),
                  pl.BlockSpec((None, 1, tn), lambda i, n: (i, 0, n))],
        out_specs=pl.BlockSpec((None, rows, tn), lambda i, n: (i, 0, n)),
        out_shape=jax.ShapeDtypeStruct((depth, rows, n6), F32),
        compiler_params=_params(2),
    )(cond, w_mod, b_mod.reshape(depth, 1, n6))


def _norm_mod_kernel(x_ref, g_ref, ml_ref, mc_ref, o_ref, *, k_shift, k_scale, n_lat, tm, has_ctx):
    is_ctx = _ctx_rows(pl.program_id(1), tm, n_lat, has_ctx)
    o_ref[...] = _modulated_norm(x_ref[...], g_ref, ml_ref, mc_ref, k_shift, k_scale, is_ctx).astype(o_ref.dtype)


def _norm_mod(x, g, mods, rows, n_lat, k_shift, k_scale, out_dtype):
    n_batch, _, d = x.shape
    tm = _pick(rows, 512, 16)
    kern = functools.partial(_norm_mod_kernel, k_shift=k_shift, k_scale=k_scale, n_lat=n_lat, tm=tm,
                             has_ctx=rows > n_lat)
    return pl.pallas_call(
        kern,
        grid=(n_batch, rows // tm),
        in_specs=[pl.BlockSpec((None, tm, d), lambda b, j: (b, j, 0)),
                  pl.BlockSpec((1, d), lambda b, j: (0, 0)),
                  pl.BlockSpec((None, 6, 1, d), lambda b, j: (b, 0, 0, 0)),
                  pl.BlockSpec((None, 6, 1, d), lambda b, j: (n_batch, 0, 0, 0))],
        out_specs=pl.BlockSpec((None, tm, d), lambda b, j: (b, j, 0)),
        out_shape=jax.ShapeDtypeStruct((n_batch, rows, d), out_dtype),
        compiler_params=_params(2),
    )(x, g.reshape(1, d), mods, mods)


def _join_norm_kernel(x_ref, c_ref, g_ref, ml_ref, mc_ref, xs_ref, h_ref, *, n_lat, tm):
    is_ctx = _ctx_rows(pl.program_id(1), tm, n_lat, True)
    rows = jnp.where(is_ctx, c_ref[...], x_ref[...])
    xs_ref[...] = rows
    h_ref[...] = _modulated_norm(rows, g_ref, ml_ref, mc_ref, K_SHIFT_TOK, K_SCALE_TOK, is_ctx).astype(h_ref.dtype)


def _join_norm(x, ctx, g, mods):
    n_batch, n_lat, d = x.shape
    n_ctx = ctx.shape[1]
    tm = _pick(math.gcd(n_lat, n_ctx), 512, 16)
    lat_tiles, ctx_tiles = n_lat // tm, n_ctx // tm
    out_spec = pl.BlockSpec((None, tm, d), lambda b, j: (b, j, 0))
    return pl.pallas_call(
        functools.partial(_join_norm_kernel, n_lat=n_lat, tm=tm),
        grid=(n_batch, lat_tiles + ctx_tiles),
        in_specs=[pl.BlockSpec((None, tm, d), lambda b, j: (b, jnp.minimum(j, lat_tiles - 1), 0)),
                  pl.BlockSpec((None, tm, d), lambda b, j: (b, jnp.maximum(j - lat_tiles, 0), 0)),
                  pl.BlockSpec((1, d), lambda b, j: (0, 0)),
                  pl.BlockSpec((None, 6, 1, d), lambda b, j: (b, 0, 0, 0)),
                  pl.BlockSpec((None, 6, 1, d), lambda b, j: (n_batch, 0, 0, 0))],
        out_specs=[out_spec, out_spec],
        out_shape=[jax.ShapeDtypeStruct((n_batch, n_lat + n_ctx, d), F32),
                   jax.ShapeDtypeStruct((n_batch, n_lat + n_ctx, d), BF16)],
        compiler_params=_params(2),
    )(x, ctx, g.reshape(1, d), mods, mods)


def _rope_tables(n_lat, n_ctx):
    rows = n_lat // GRID_W
    row = jnp.repeat(jnp.arange(rows, dtype=F32), GRID_W)
    col = jnp.tile(jnp.arange(GRID_W, dtype=F32), rows)

    def angles(dim):
        n = dim // 4
        inv = ROPE_THETA ** (-jnp.arange(n, dtype=F32) / n)
        return jnp.concatenate([row[:, None] * inv, col[:, None] * inv], axis=-1)

    a64, a128 = angles(A_DIM), angles(C_DIM)
    lane = jnp.arange(LANES)
    c64, s64 = jnp.cos(a64)[:, lane % 32], jnp.sin(a64)[:, lane % 32]
    first_half = (lane % 64) < 32
    sa64 = jnp.where(first_half, -s64, 0.0)
    sb64 = jnp.where(first_half, 0.0, s64)
    c128 = jnp.cos(a128)[:, lane % 64]
    s128 = jnp.sin(a128)[:, lane % 64] * jnp.where(lane < 64, -1.0, 1.0)
    lat = jnp.stack([c64, sa64, sb64, c128, s128])
    ident = jnp.zeros((5, n_ctx, LANES), F32).at[jnp.array([0, 3])].set(1.0)
    return jnp.concatenate([lat, ident], axis=1)


def _proj_epilogue(kind, x, tab_ref, qkg_ref):
    name = kind[0]
    if name == "none":
        return x
    if name == "silu":
        return _silu(x)
    if name == "rope64":
        y = (x * tab_ref[0] + pltpu.roll(x, 96, 1) * tab_ref[1] + pltpu.roll(x, 32, 1) * tab_ref[2])
        return y * kind[1]
    if name == "norm_rope128":
        x = _rms(x, EPS) * qkg_ref[kind[2]:kind[2] + 1, :]
    y = x * tab_ref[3] + pltpu.roll(x, 64, 1) * tab_ref[4]
    return y * kind[1]


def _proj_kernel(h_ref, w_ref, tab_ref, qkg_ref, o_ref, acc_ref, *, groups, n_tiles):
    n = pl.program_id(2)
    slot = lax.rem(n, 2)

    def matmul():
        acc_ref[slot] = jnp.dot(h_ref[...], w_ref[...], preferred_element_type=F32)

    def epilogue(kinds):
        prev = acc_ref.at[1 - slot]
        for c, kind in enumerate(kinds):
            sl = slice(c * LANES, (c + 1) * LANES)
            o_ref[:, sl] = _proj_epilogue(kind, prev[:, sl], tab_ref, qkg_ref).astype(o_ref.dtype)

    pl.when(n == 0)(matmul)
    for tiles, kinds in groups:
        prev_in_group = functools.reduce(jnp.logical_or, [n - 1 == t for t in tiles])

        @pl.when(prev_in_group & (n < n_tiles))
        def _(kinds=kinds):
            epilogue(kinds)
            matmul()

        if n_tiles - 1 in tiles:
            pl.when(n == n_tiles)(functools.partial(epilogue, kinds))


def _proj_layout(d):
    a_heads, b_heads, c_heads = d // 256, d // 512, d // 256
    c_kv = c_heads // C_GROUP
    segs = [("aq", a_heads, ("rope64", A_DIM ** -0.5 * LOG2E)), ("ak", a_heads, ("rope64", 1.0)),
            ("av", a_heads, ("none",)),
            ("bq", b_heads, ("rope128", 1.0)), ("bk", b_heads, ("rope128", B_QK ** -0.5)),
            ("bv", 2 * b_heads, ("none",)), ("bg", 2 * b_heads, ("silu",)),
            ("cq", c_heads, ("norm_rope128", C_DIM ** -0.5 * LOG2E, 0)), ("ck", c_kv, ("norm_rope128", 1.0, 1)),
            ("cv", c_kv, ("none",))]
    kinds, off = [], {}
    for name, nblk, kind in segs:
        off[name] = len(kinds)
        kinds += [kind] * nblk
    return kinds, off


def _proj(h, w, tabs, qkg, kinds):
    n_batch, rows, d = h.shape
    width = w.shape[1]
    tm = _pick(rows, 1152, 16)
    tn = _pick(width, 768, LANES)
    per = tn // LANES
    by_kinds = {}
    for t in range(width // tn):
        by_kinds.setdefault(tuple(kinds[t * per:(t + 1) * per]), []).append(t)
    groups = tuple((tuple(tiles), ks) for ks, tiles in by_kinds.items())
    n_tiles = width // tn
    return pl.pallas_call(
        functools.partial(_proj_kernel, groups=groups, n_tiles=n_tiles),
        grid=(n_batch, rows // tm, n_tiles + 1),
        in_specs=[pl.BlockSpec((None, tm, d), lambda b, j, n: (b, j, 0)),
                  pl.BlockSpec((d, tn), lambda b, j, n: (0, jnp.minimum(n, n_tiles - 1))),
                  pl.BlockSpec((5, tm, LANES), lambda b, j, n: (0, j, 0)),
                  pl.BlockSpec((2, LANES), lambda b, j, n: (0, 0))],
        out_specs=pl.BlockSpec((None, tm, tn), lambda b, j, n: (b, j, jnp.maximum(n - 1, 0))),
        out_shape=jax.ShapeDtypeStruct((n_batch, rows, width), BF16),
        scratch_shapes=[pltpu.VMEM((2, tm, tn), F32)],
        compiler_params=_params(3),
    )(h, w, tabs, qkg)


def _scores(q, k):
    return lax.dot_general(q, k, (((1,), (1,)), ((), ())), preferred_element_type=F32)


def _softmax_pv_rowmax(q, k, vx):
    s = _scores(q, k)
    e = jnp.exp2(s - jnp.max(s, axis=-1, keepdims=True)).astype(BF16)
    r = jnp.dot(e, vx, preferred_element_type=F32)
    return r[:, :LANES] / r[:, LANES:LANES + 1]


def _bound_shift(q, ksq_max):
    qf = q.astype(F32)
    return BOUND_SLACK * jnp.sqrt(jnp.sum(qf * qf, axis=-1, keepdims=True) * ksq_max)


def _attn_kernel(q_ref, k_ref, v_ref, lam_ref, g_ref, o_ref, vx_ref, ksq_ref, *,
                 diff, lam_init, n_lat_tiles, n_lat, has_ctx, heads, kv_shared):
    i = pl.program_id(2)
    n_kv = 1 if kv_shared else heads
    n_keys = vx_ref.shape[1]

    def lanes(j):
        return slice(j * LANES, (j + 1) * LANES)

    @pl.when(i == 0)
    def _():
        for j in range(n_kv):
            vx_ref[j, :, :LANES] = v_ref[:, lanes(j)]
            vx_ref[j, :, LANES:] = jnp.ones((n_keys, LANES), BF16)
            k = k_ref[:, lanes(j)]
            ksq = jnp.dot(k * k, jnp.ones((LANES, LANES), BF16), preferred_element_type=F32)
            ksq_ref[j] = jnp.broadcast_to(jnp.max(ksq, axis=0, keepdims=True), (SUBLANES, LANES))

    def run(key_rows, nq):
        def operands(hh):
            j = 0 if kv_shared else hh
            q = q_ref[:nq, lanes(hh)]
            if diff:
                qf = q.astype(F32)
                lane = lax.broadcasted_iota(jnp.int32, qf.shape, 1)
                qs = (jnp.where(lane < A_DIM, qf, 0.0).astype(BF16), jnp.where(lane >= A_DIM, qf, 0.0).astype(BF16))
            else:
                qs = (q,)
            return qs, k_ref[key_rows, lanes(j)], vx_ref[j, key_rows, :], ksq_ref[j, 0:1, 0:1]

        def finish(hh, outs):
            if diff:
                lv = lam_ref[...]
                lam = (jnp.exp(jnp.sum(lv[0:1] * lv[1:2], axis=-1, keepdims=True))
                       - jnp.exp(jnp.sum(lv[2:3] * lv[3:4], axis=-1, keepdims=True)) + lam_init)
                o = _rms(outs[0] - lam * outs[1], EPS) * g_ref[...] * (1.0 - lam_init)
            else:
                o = outs[0]
            o_ref[:nq, lanes(hh)] = o.astype(o_ref.dtype)

        ops = [(hh, qq, k, vx, ksq) for hh in range(heads) for qs, k, vx, ksq in [operands(hh)] for qq in qs]
        shifted = [_scores(qq, k) - _bound_shift(qq, ksq) for _, qq, k, _, ksq in ops]
        exps = [jnp.exp2(s).astype(BF16) for s in shifted]
        prods = [jnp.dot(e, op[3], preferred_element_type=F32) for e, op in zip(exps, ops)]
        parts = [[] for _ in range(heads)]
        for op, r in zip(ops, prods):
            parts[op[0]].append((r[:, :LANES], r[:, LANES:LANES + 1]))
        den_min = functools.reduce(jnp.minimum, [den for head in parts for _, den in head])
        precise = jnp.min(den_min) >= DEN_FLOOR

        @pl.when(precise)
        def _():
            for hh in range(heads):
                finish(hh, [num / den for num, den in parts[hh]])

        @pl.when(jnp.logical_not(precise))
        def _():
            for hh in range(heads):
                qs, k, vx, _ = operands(hh)
                finish(hh, [_softmax_pv_rowmax(qq, k, vx) for qq in qs])

    tq = q_ref.shape[0]
    if not has_ctx:
        run(slice(None), tq)
        return

    @pl.when(i < n_lat_tiles)
    def _():
        run(slice(None), tq)

    @pl.when(i >= n_lat_tiles)
    def _():
        run(slice(n_lat, None), min(tq, n_keys - n_lat))


def _query_tile(n_lat, n_ctx, has_ctx):
    tq = _pick(n_lat, 512, 16)
    assert not has_ctx or n_ctx % tq == 0 or n_ctx < tq
    return tq


def _attention(p, lam_vec, subln_g, rows, n_lat, n_ctx, n_heads, q_off, k_off, v_off, kv_group, diff, lam_init):
    n_batch, n_keys, _ = p.shape
    has_ctx = rows > n_lat
    tq = _query_tile(n_lat, n_ctx, has_ctx)
    kv_shared = kv_group > 1
    heads = kv_group if kv_shared else 2
    assert n_heads % heads == 0 and q_off % heads == 0 and (kv_shared or (k_off % heads == 0 and v_off % heads == 0))
    kv_w = LANES if kv_shared else heads * LANES
    kv_blk = (lambda off: (lambda b, h, i: (b, 0, off + h))) if kv_shared else \
             (lambda off: (lambda b, h, i: (b, 0, off // heads + h)))
    kern = functools.partial(_attn_kernel, diff=diff, lam_init=lam_init, n_lat_tiles=n_lat // tq,
                             n_lat=n_lat, has_ctx=has_ctx, heads=heads, kv_shared=kv_shared)
    n_kv = 1 if kv_shared else heads
    return pl.pallas_call(
        kern,
        grid=(n_batch, n_heads // heads, pl.cdiv(rows, tq)),
        in_specs=[pl.BlockSpec((None, tq, heads * LANES), lambda b, h, i: (b, i, q_off // heads + h)),
                  pl.BlockSpec((None, n_keys, kv_w), kv_blk(k_off)),
                  pl.BlockSpec((None, n_keys, kv_w), kv_blk(v_off)),
                  pl.BlockSpec(lam_vec.shape, lambda b, h, i: (0, 0)),
                  pl.BlockSpec((1, LANES), lambda b, h, i: (0, 0))],
        out_specs=pl.BlockSpec((None, tq, heads * LANES), lambda b, h, i: (b, i, h)),
        out_shape=jax.ShapeDtypeStruct((n_batch, rows, n_heads * LANES), BF16),
        scratch_shapes=[pltpu.VMEM((n_kv, n_keys, 2 * LANES), BF16), pltpu.VMEM((n_kv, SUBLANES, LANES), F32)],
        compiler_params=_params(3),
    )(p, p, p, lam_vec, subln_g.reshape(1, LANES))


def _ret_decay(lgf, lgb, t, pf, pb):
    df = (t - pf).astype(F32)
    db = (pb - t).astype(F32)
    fwd = jnp.where(df >= 0, jnp.exp(lgf * jnp.maximum(df, 0.0)), 0.0)
    bwd = jnp.where(db >= 0, jnp.exp(lgb * jnp.maximum(db, 0.0)), 0.0)
    return fwd + bwd


def _ret_kernel(ld_ref, q_ref, k_ref, v_ref, g_ref, o_ref, d_ref, *, tq, n_keys, n_lat, n_heads, has_ctx, heads):
    hp, i, b = pl.program_id(0), pl.program_id(1), pl.program_id(2)
    n_ctx = n_keys - n_lat
    n_lat_tiles = n_lat // tq
    t_loc = lax.broadcasted_iota(jnp.int32, (tq, 1), 0)

    def log_decays(j):
        return ld_ref[hp * heads + j], ld_ref[n_heads + hp * heads + j]

    def lat_decay():
        col = lax.broadcasted_iota(jnp.int32, (1, n_keys), 1)
        pf = jnp.where(col < n_lat, col, col - n_keys)
        for j in range(heads):
            d_ref[j] = _ret_decay(*log_decays(j), i * tq + t_loc, pf, col)

    def ctx_decay():
        col = lax.broadcasted_iota(jnp.int32, (1, n_ctx), 1)
        for j in range(heads):
            d_ref[j, :, :n_ctx] = _ret_decay(*log_decays(j), (i - n_lat_tiles) * tq + t_loc, col, col)

    def run(key_rows, n_cols, nq):
        def qk(j):
            sl = slice(j * B_QK, (j + 1) * B_QK)
            return lax.dot_general(q_ref[:nq, sl], k_ref[key_rows, sl], (((1,), (1,)), ((), ())),
                                   preferred_element_type=F32)

        scores = [qk(j) for j in range(heads)]
        weighted = [(s * d_ref[j, :nq, :n_cols]).astype(BF16) for j, s in enumerate(scores)]
        for j, w in enumerate(weighted):
            sl = slice(j * B_V, (j + 1) * B_V)
            o = jnp.dot(w, v_ref[key_rows, sl], preferred_element_type=F32)
            mu = jnp.mean(o, axis=-1, keepdims=True)
            var = jnp.mean(jnp.square(o - mu), axis=-1, keepdims=True)
            o_ref[:nq, sl] = ((o - mu) * lax.rsqrt(var + GN_EPS) * g_ref[:nq, sl].astype(F32)).astype(o_ref.dtype)

    if not has_ctx:
        pl.when(b == 0)(lat_decay)
        run(slice(None), n_keys, tq)
        return

    pl.when((b == 0) & (i < n_lat_tiles))(lat_decay)
    pl.when((b == 0) & (i >= n_lat_tiles))(ctx_decay)

    @pl.when(i < n_lat_tiles)
    def _():
        run(slice(None), n_keys, tq)

    @pl.when(i >= n_lat_tiles)
    def _():
        run(slice(n_lat, None), n_ctx, min(tq, n_ctx))


def _retention(p, log_decay, rows, n_lat, n_ctx, n_heads, off):
    n_batch, n_keys, _ = p.shape
    has_ctx = rows > n_lat
    tq = _query_tile(n_lat, n_ctx, has_ctx)
    heads = 2 if n_heads % 2 == 0 else 1
    v_off, g_off = off["bv"] // 2, off["bg"] // 2
    assert all(o % heads == 0 for o in (off["bq"], off["bk"], v_off, g_off))
    kern = functools.partial(_ret_kernel, tq=tq, n_keys=n_keys, n_lat=n_lat, n_heads=n_heads, has_ctx=has_ctx,
                             heads=heads)
    grid_spec = pltpu.PrefetchScalarGridSpec(
        num_scalar_prefetch=1,
        grid=(n_heads // heads, pl.cdiv(rows, tq), n_batch),
        in_specs=[pl.BlockSpec((None, tq, heads * B_QK), lambda h, i, b, ld: (b, i, off["bq"] // heads + h)),
                  pl.BlockSpec((None, n_keys, heads * B_QK), lambda h, i, b, ld: (b, 0, off["bk"] // heads + h)),
                  pl.BlockSpec((None, n_keys, heads * B_V), lambda h, i, b, ld: (b, 0, v_off // heads + h)),
                  pl.BlockSpec((None, tq, heads * B_V), lambda h, i, b, ld: (b, i, g_off // heads + h))],
        out_specs=pl.BlockSpec((None, tq, heads * B_V), lambda h, i, b, ld: (b, i, h)),
        scratch_shapes=[pltpu.VMEM((heads, tq, n_keys), F32)])
    return pl.pallas_call(
        kern, grid_spec=grid_spec,
        out_shape=jax.ShapeDtypeStruct((n_batch, rows, n_heads * B_V), BF16),
        compiler_params=_params(3),
    )(log_decay.reshape(-1).astype(F32), p, p, p, p)


def _merge_kernel(h_ref, oa_ref, ob_ref, oc_ref, ga_ref, gb_ref, gc_ref, wb_ref, y_ref):
    h = h_ref[...]
    y = None
    for idx, (o_ref, wg_ref) in enumerate(((oa_ref, ga_ref), (ob_ref, gb_ref), (oc_ref, gc_ref))):
        gate = jax.nn.sigmoid(jnp.dot(h, wg_ref[...], preferred_element_type=F32))
        term = gate * jnp.dot(o_ref[...], wb_ref[idx], preferred_element_type=F32)
        y = term if y is None else y + term
    y_ref[...] = y.astype(y_ref.dtype)


def _merge(h, outs, w_gate, w_branch, rows):
    n_batch, _, d = h.shape
    bw = w_branch.shape[1]
    tm = _pick(rows, 1152, 16)
    tn = _pick(d, 512, LANES)
    nn = d // tn
    o_spec = pl.BlockSpec((None, tm, bw), lambda b, j, n: (b, j, 0))
    g_specs = [pl.BlockSpec((d, tn), functools.partial(lambda b, j, n, k: (0, k * nn + n), k=k)) for k in range(3)]
    return pl.pallas_call(
        _merge_kernel,
        grid=(n_batch, rows // tm, nn),
        in_specs=[pl.BlockSpec((None, tm, d), lambda b, j, n: (b, j, 0)), o_spec, o_spec, o_spec,
                  *g_specs, pl.BlockSpec((3, bw, tn), lambda b, j, n: (0, 0, n))],
        out_specs=pl.BlockSpec((None, tm, tn), lambda b, j, n: (b, j, n)),
        out_shape=jax.ShapeDtypeStruct((n_batch, rows, d), BF16),
        compiler_params=_params(3),
    )(h, *outs, w_gate, w_gate, w_gate, w_branch)


def _out_resid_kernel(y_ref, w_ref, x_ref, g_ref, gn_ref, ml_ref, mc_ref, o_ref, hn_ref, *, tm, n_lat, has_ctx):
    is_ctx = _ctx_rows(pl.program_id(1), tm, n_lat, has_ctx)
    f = jnp.dot(y_ref[...], w_ref[...], preferred_element_type=F32)
    x_new = _gated_residual(x_ref[...], f, g_ref, ml_ref, mc_ref, K_GATE_TOK, is_ctx)
    o_ref[...] = x_new
    hn_ref[...] = _modulated_norm(x_new, gn_ref, ml_ref, mc_ref, K_SHIFT_CH, K_SCALE_CH, is_ctx).astype(hn_ref.dtype)


def _out_resid(y, w_out, x, g, g_next, mods, rows, n_lat, next_dtype):
    n_batch, _, d = y.shape
    tm = _pick(rows, 512, 16)
    kern = functools.partial(_out_resid_kernel, tm=tm, n_lat=n_lat, has_ctx=rows > n_lat)
    row_spec = pl.BlockSpec((None, tm, d), lambda b, j: (b, j, 0))
    vec_spec = pl.BlockSpec((1, d), lambda b, j: (0, 0))
    return pl.pallas_call(
        kern,
        grid=(n_batch, rows // tm),
        in_specs=[row_spec, pl.BlockSpec((d, d), lambda b, j: (0, 0)), row_spec, vec_spec, vec_spec,
                  pl.BlockSpec((None, 6, 1, d), lambda b, j: (b, 0, 0, 0)),
                  pl.BlockSpec((None, 6, 1, d), lambda b, j: (n_batch, 0, 0, 0))],
        out_specs=[row_spec, row_spec],
        out_shape=[jax.ShapeDtypeStruct((n_batch, rows, d), F32), jax.ShapeDtypeStruct((n_batch, rows, d), next_dtype)],
        compiler_params=_params(2),
    )(y, w_out, x, g.reshape(1, d), g_next.reshape(1, d), mods, mods)


def _swiglu_steps(s, nf, h_ref, wg_ref, wu_ref, wo_ref, a_ref, o_ref):
    slot = lax.rem(s, 2)

    def activate():
        h = h_ref[...]
        gate = jnp.dot(h, wg_ref[...], preferred_element_type=F32)
        up = jnp.dot(h, wu_ref[...], preferred_element_type=F32)
        a_ref[slot] = (_silu(gate) * up).astype(BF16)

    def project():
        o_ref[...] += jnp.dot(a_ref[1 - slot], wo_ref[...], preferred_element_type=F32)

    @pl.when(s == 0)
    def _():
        o_ref[...] = jnp.zeros_like(o_ref)
        activate()

    @pl.when((s > 0) & (s < nf))
    def _():
        project()
        activate()

    @pl.when(s == nf)
    def _():
        project()


def _ffn_kernel(*refs, tm, nf, n_lat, has_ctx, has_next):
    if has_next:
        (h_ref, wg_ref, wu_ref, wo_ref, x_ref, g_ref, ml_ref, mc_ref, gn_ref, mln_ref, mcn_ref,
         o_ref, hn_ref, a_ref) = refs
    else:
        h_ref, wg_ref, wu_ref, wo_ref, x_ref, g_ref, ml_ref, mc_ref, o_ref, a_ref = refs
    s = pl.program_id(2)
    _swiglu_steps(s, nf, h_ref, wg_ref, wu_ref, wo_ref, a_ref, o_ref)

    @pl.when(s == nf)
    def _():
        is_ctx = _ctx_rows(pl.program_id(1), tm, n_lat, has_ctx)
        x_new = _gated_residual(x_ref[...], o_ref[...], g_ref, ml_ref, mc_ref, K_GATE_CH, is_ctx)
        o_ref[...] = x_new
        if has_next:
            hn_ref[...] = _modulated_norm(x_new, gn_ref, mln_ref, mcn_ref, K_SHIFT_TOK, K_SCALE_TOK,
                                          is_ctx).astype(hn_ref.dtype)


def _ffn_dense(h, w_in, w_out, x, g, mods, rows, n_lat, g_next, mods_next):
    n_batch, _, d = h.shape
    ff = w_out.shape[0]
    tm = _pick(rows, 576, 16)
    tf = _pick(ff, 512, LANES)
    nf = ff // tf
    has_next = g_next is not None
    kern = functools.partial(_ffn_kernel, tm=tm, nf=nf, n_lat=n_lat, has_ctx=rows > n_lat, has_next=has_next)
    row_spec = pl.BlockSpec((None, tm, d), lambda b, j, f: (b, j, 0))
    vec_spec = pl.BlockSpec((1, d), lambda b, j, f: (0, 0))
    mod_specs = [pl.BlockSpec((None, 6, 1, d), lambda b, j, f: (b, 0, 0, 0)),
                 pl.BlockSpec((None, 6, 1, d), lambda b, j, f: (n_batch, 0, 0, 0))]
    in_specs = [row_spec,
                pl.BlockSpec((d, tf), lambda b, j, f: (0, jnp.minimum(f, nf - 1))),
                pl.BlockSpec((d, tf), lambda b, j, f: (0, nf + jnp.minimum(f, nf - 1))),
                pl.BlockSpec((tf, d), lambda b, j, f: (jnp.maximum(f - 1, 0), 0)),
                row_spec, vec_spec, *mod_specs]
    args = [h, w_in, w_in, w_out, x, g.reshape(1, d), mods, mods]
    out_specs, out_shape = [row_spec], [jax.ShapeDtypeStruct((n_batch, rows, d), F32)]
    if has_next:
        in_specs += [vec_spec, *mod_specs]
        args += [g_next.reshape(1, d), mods_next, mods_next]
        out_specs.append(row_spec)
        out_shape.append(jax.ShapeDtypeStruct((n_batch, rows, d), BF16))
    outs = pl.pallas_call(
        kern,
        grid=(n_batch, rows // tm, nf + 1),
        in_specs=in_specs, out_specs=out_specs, out_shape=out_shape,
        scratch_shapes=[pltpu.VMEM((2, tm, tf), BF16)],
        compiler_params=_params(3),
    )(*args)
    return (outs[0], outs[1]) if has_next else (outs[0], None)


def _router_kernel(h_ref, w_ref, idx_ref, wt_ref, *, n_experts):
    logits = jnp.dot(h_ref[...].astype(BF16), w_ref[...], preferred_element_type=F32)
    lane = lax.broadcasted_iota(jnp.int32, logits.shape, 1)
    neg = jnp.float32(-jnp.inf)
    lg = jnp.where(lane < n_experts, logits, neg)
    m1 = jnp.max(lg, axis=-1, keepdims=True)
    i1 = jnp.min(jnp.where(lg == m1, lane, LANES), axis=-1, keepdims=True)
    lg2 = jnp.where(lane == i1, neg, lg)
    m2 = jnp.max(lg2, axis=-1, keepdims=True)
    i2 = jnp.min(jnp.where(lg2 == m2, lane, LANES), axis=-1, keepdims=True)
    e2 = jnp.exp(m2 - m1)
    den = 1.0 + e2
    idx_ref[...] = jnp.where(lane == 0, i1, jnp.where(lane == 1, i2, 0))
    wt_ref[...] = jnp.where(lane == 0, 1.0 / den, jnp.where(lane == 1, e2 / den, 0.0))


def _router(h, w_router):
    n_tok, d = h.shape
    n_experts = w_router.shape[1]
    tm = _pick(n_tok, 1024, 8)
    w_pad = jnp.zeros((d, LANES), BF16).at[:, :n_experts].set(w_router.astype(BF16))
    out_spec = pl.BlockSpec((tm, LANES), lambda i: (i, 0))
    return pl.pallas_call(
        functools.partial(_router_kernel, n_experts=n_experts),
        grid=(n_tok // tm,),
        in_specs=[pl.BlockSpec((tm, d), lambda i: (i, 0)), pl.BlockSpec((d, LANES), lambda i: (0, 0))],
        out_specs=[out_spec, out_spec],
        out_shape=[jax.ShapeDtypeStruct((n_tok, LANES), jnp.int32), jax.ShapeDtypeStruct((n_tok, LANES), F32)],
        compiler_params=_params(1),
    )(h, w_pad)


def _route_plan(top_idx, n_experts, tm, n_tiles):
    n_tok = top_idx.shape[0]
    e_flat = top_idx.reshape(-1)
    onehot = (e_flat[:, None] == jnp.arange(n_experts, dtype=jnp.int32)[None, :]).astype(jnp.int32)
    csum = jnp.cumsum(onehot, axis=0)
    rank = jnp.sum((csum - onehot) * onehot, axis=1)
    tiles_e = (csum[-1] + tm - 1) // tm
    tiles_end = jnp.cumsum(tiles_e)
    pos = (tiles_end - tiles_e)[e_flat] * tm + rank
    n_used = tiles_end[-1]
    tile_ids = jnp.arange(n_tiles, dtype=jnp.int32)
    tile_expert = jnp.sum((tile_ids[:, None] >= tiles_end[None, :]).astype(jnp.int32), axis=1)
    tile_expert = jnp.minimum(tile_expert, n_experts - 1)
    last_expert = tile_expert[jnp.maximum(n_used - 1, 0)]
    tile_expert = jnp.where(tile_ids < n_used, tile_expert, last_expert)
    row_token = jnp.zeros((n_tiles * tm,), jnp.int32).at[pos].set(jnp.arange(TOP_K * n_tok, dtype=jnp.int32) // TOP_K)
    return row_token, pos.reshape(n_tok, TOP_K).astype(jnp.int32), tile_expert, n_used.reshape(1).astype(jnp.int32)


def _gather_kernel(tok_ref, tokn_ref, src_ref, o_ref, buf_ref, sems, *, tg):
    step = pl.program_id(0)
    slot = lax.rem(step, 2)

    def start_rows(t_ref, sl):
        def body(r, carry):
            pltpu.make_async_copy(src_ref.at[pl.ds(t_ref[0, r], 1)], buf_ref.at[sl, pl.ds(r, 1)], sems.at[sl]).start()
            return carry
        lax.fori_loop(0, tg, body, 0, unroll=8)

    pl.when(step == 0)(lambda: start_rows(tok_ref, slot))
    pl.when(step + 1 < pl.num_programs(0))(lambda: start_rows(tokn_ref, 1 - slot))
    pltpu.make_async_copy(src_ref.at[pl.ds(0, tg)], buf_ref.at[slot], sems.at[slot]).wait()
    o_ref[...] = buf_ref[slot].astype(o_ref.dtype)


def _gather_rows(src, row_token, tg, out_dtype):
    n_rows = row_token.shape[0]
    d = src.shape[1]
    n_steps = n_rows // tg
    tok = row_token.reshape(n_steps, 1, tg)
    tok_blk = (None, 1, tg)
    return pl.pallas_call(
        functools.partial(_gather_kernel, tg=tg),
        grid=(n_steps,),
        in_specs=[pl.BlockSpec(tok_blk, lambda i: (i, 0, 0), memory_space=pltpu.SMEM),
                  pl.BlockSpec(tok_blk, lambda i: (jnp.minimum(i + 1, n_steps - 1), 0, 0), memory_space=pltpu.SMEM),
                  pl.BlockSpec(memory_space=pl.ANY)],
        out_specs=pl.BlockSpec((tg, d), lambda i: (i, 0)),
        out_shape=jax.ShapeDtypeStruct((n_rows, d), out_dtype),
        scratch_shapes=[pltpu.VMEM((2, tg, d), src.dtype), pltpu.SemaphoreType.DMA((2,))],
        compiler_params=_params(1),
    )(tok, tok, src)


def _moe_ffn_kernel(te_ref, nv_ref, xs_ref, wg_ref, wu_ref, wo_ref, o_ref):
    i, f = pl.program_id(0), pl.program_id(1)
    valid = i < nv_ref[0]

    @pl.when(f == 0)
    def _():
        o_ref[...] = jnp.zeros_like(o_ref)

    @pl.when(valid)
    def _():
        xb = xs_ref[...]
        gate = jnp.dot(xb, wg_ref[...], preferred_element_type=F32)
        up = jnp.dot(xb, wu_ref[...], preferred_element_type=F32)
        o_ref[...] += jnp.dot((_silu(gate) * up).astype(BF16), wo_ref[...], preferred_element_type=F32)


def _moe_ffn(xs, w_in, w_out, tile_expert, n_used, tm):
    n_rows, d = xs.shape
    ff = w_out.shape[1]
    tf = _pick(ff, 512, LANES)
    nf = ff // tf

    def f_eff(i, f, nv):
        return jnp.where(i < nv[0], f, nf - 1)

    grid_spec = pltpu.PrefetchScalarGridSpec(
        num_scalar_prefetch=2,
        grid=(n_rows // tm, nf),
        in_specs=[pl.BlockSpec((tm, d), lambda i, f, te, nv: (jnp.minimum(i, jnp.maximum(nv[0] - 1, 0)), 0)),
                  pl.BlockSpec((None, d, tf), lambda i, f, te, nv: (te[i], 0, f_eff(i, f, nv))),
                  pl.BlockSpec((None, d, tf), lambda i, f, te, nv: (te[i], 0, nf + f_eff(i, f, nv))),
                  pl.BlockSpec((None, tf, d), lambda i, f, te, nv: (te[i], f_eff(i, f, nv), 0))],
        out_specs=pl.BlockSpec((tm, d), lambda i, f, te, nv: (i, 0)))
    return pl.pallas_call(
        _moe_ffn_kernel, grid_spec=grid_spec,
        out_shape=jax.ShapeDtypeStruct((n_rows, d), F32),
        compiler_params=_params(2),
    )(tile_expert, n_used, xs, w_in, w_in, w_out)


def _combine_kernel(pos_ref, posn_ref, ys_ref, wt_ref, x_ref, g_ref, ml_ref, mc_ref, o_ref, buf_ref, sems, *,
                    tc, n_lat, has_ctx):
    step = pl.program_id(0) * pl.num_programs(1) + pl.program_id(1)
    n_steps = pl.num_programs(0) * pl.num_programs(1)
    slot = lax.rem(step, 2)

    def start_rows(p_ref, sl):
        def body(r, carry):
            for k in range(TOP_K):
                pltpu.make_async_copy(ys_ref.at[pl.ds(p_ref[0, TOP_K * r + k], 1)],
                                      buf_ref.at[sl, k, pl.ds(r, 1)], sems.at[sl]).start()
            return carry
        lax.fori_loop(0, tc, body, 0, unroll=4)

    pl.when(step == 0)(lambda: start_rows(pos_ref, slot))
    pl.when(step + 1 < n_steps)(lambda: start_rows(posn_ref, 1 - slot))
    for k in range(TOP_K):
        pltpu.make_async_copy(ys_ref.at[pl.ds(0, tc)], buf_ref.at[slot, k], sems.at[slot]).wait()
    wt = wt_ref[...]
    f = wt[:, 0:1] * buf_ref[slot, 0]
    for k in range(1, TOP_K):
        f = f + wt[:, k:k + 1] * buf_ref[slot, k]
    is_ctx = _ctx_rows(pl.program_id(1), tc, n_lat, has_ctx)
    o_ref[...] = _gated_residual(x_ref[...], f, g_ref, ml_ref, mc_ref, K_GATE_CH, is_ctx)


def _combine(ys, pos, wts, x, g, mods, rows, n_lat):
    n_batch, _, d = x.shape
    tc = _pick(rows, 256, 8)
    nj = rows // tc
    n_steps = n_batch * nj
    kern = functools.partial(_combine_kernel, tc=tc, n_lat=n_lat, has_ctx=rows > n_lat)
    row_spec = pl.BlockSpec((None, tc, d), lambda b, j: (b, j, 0))
    pos_blk = (None, 1, TOP_K * tc)
    pos_steps = pos.reshape(n_steps, 1, TOP_K * tc)
    return pl.pallas_call(
        kern,
        grid=(n_batch, nj),
        in_specs=[pl.BlockSpec(pos_blk, lambda b, j: (b * nj + j, 0, 0), memory_space=pltpu.SMEM),
                  pl.BlockSpec(pos_blk, lambda b, j: (jnp.minimum(b * nj + j + 1, n_steps - 1), 0, 0),
                               memory_space=pltpu.SMEM),
                  pl.BlockSpec(memory_space=pl.ANY),
                  pl.BlockSpec((tc, LANES), lambda b, j: (b * nj + j, 0)),
                  row_spec,
                  pl.BlockSpec((1, d), lambda b, j: (0, 0)),
                  pl.BlockSpec((None, 6, 1, d), lambda b, j: (b, 0, 0, 0)),
                  pl.BlockSpec((None, 6, 1, d), lambda b, j: (n_batch, 0, 0, 0))],
        out_specs=row_spec,
        out_shape=jax.ShapeDtypeStruct((n_batch, rows, d), F32),
        scratch_shapes=[pltpu.VMEM((2, TOP_K, tc, d), F32), pltpu.SemaphoreType.DMA((2,))],
        compiler_params=_params(2),
    )(pos_steps, pos_steps, ys, wts, x, g.reshape(1, d), mods, mods)


def _ffn_moe(h, w_router, w_in, w_out, x, g, mods, rows, n_lat):
    n_batch, _, d = h.shape
    n_tok = n_batch * rows
    n_experts = w_router.shape[1]
    tm = min(768, max(16, (TOP_K * n_tok // n_experts) // 16 * 16))
    n_tiles = (TOP_K * n_tok + n_experts * (tm - 1)) // tm
    h_flat = h.reshape(n_tok, d)
    idx, wts = _router(h_flat, w_router)
    row_token, pos, tile_expert, n_used = _route_plan(idx[:, :TOP_K], n_experts, tm, n_tiles)
    xs = _gather_rows(h_flat, row_token, tm, BF16)
    ys = _moe_ffn(xs, w_in, w_out, tile_expert, n_used, tm)
    return _combine(ys, pos, wts, x, g, mods, rows, n_lat)


def kernel(x, c, ctx, c_ctx, w_mod, b_mod, norm_g, w_in, qk_norm_g, diff_lambda, diff_subln_g, ret_log_decay,
           w_branch, w_branch_gate, w_out, ffn_w_in, ffn_w_out, moe_router, moe_w_in, moe_w_out):
    n_batch, n_lat, d = x.shape
    n_ctx = ctx.shape[1]
    depth = w_mod.shape[0]
    a_heads, b_heads, c_heads = d // 256, d // 512, d // 256
    kinds, off = _proj_layout(d)

    mod_rows = (n_batch + 1 + 7) // 8 * 8
    cond = jnp.zeros((mod_rows, d), F32).at[:n_batch].set(c).at[n_batch].set(c_ctx)
    mods_all = _modvec(cond, w_mod, b_mod).reshape(depth, mod_rows, 6, 1, d)
    tabs = _rope_tables(n_lat, n_ctx)

    n_all = n_lat + n_ctx
    xs, h = _join_norm(x, ctx, norm_g[0, 0], mods_all[0])
    for i in range(depth):
        last = i == depth - 1
        rows = n_lat if last else n_all
        lam_init = 0.8 - 0.6 * math.exp(-0.3 * i)
        mods = mods_all[i]
        dense = i % 2 == 0
        j = i // 2
        p = _proj(h, w_in[i].astype(BF16), tabs, qk_norm_g[i], kinds)
        o_a = _attention(p, diff_lambda[i], diff_subln_g[i], rows, n_lat, n_ctx, a_heads,
                         off["aq"], off["ak"], off["av"], 1, True, lam_init)
        o_b = _retention(p, ret_log_decay[i], rows, n_lat, n_ctx, b_heads, off)
        o_c = _attention(p, diff_lambda[i], diff_subln_g[i], rows, n_lat, n_ctx, c_heads,
                         off["cq"], off["ck"], off["cv"], C_GROUP, False, lam_init)
        y = _merge(h, (o_a, o_b, o_c), w_branch_gate[i].astype(BF16), w_branch[i].astype(BF16), rows)
        xs, h2 = _out_resid(y, w_out[i].astype(BF16), xs, norm_g[i, 1], norm_g[i, 2], mods, rows, n_lat,
                            BF16 if dense else F32)
        if dense:
            g_next, mods_next = (None, None) if last else (norm_g[i + 1, 0], mods_all[i + 1])
            xs, h = _ffn_dense(h2, ffn_w_in[j].astype(BF16), ffn_w_out[j].astype(BF16), xs, norm_g[i, 3], mods,
                               rows, n_lat, g_next, mods_next)
        else:
            xs = _ffn_moe(h2, moe_router[j], moe_w_in[j].astype(BF16), moe_w_out[j].astype(BF16), xs,
                          norm_g[i, 3], mods, rows, n_lat)
            if not last:
                h = _norm_mod(xs, norm_g[i + 1, 0], mods_all[i + 1], n_all, n_lat, K_SHIFT_TOK, K_SCALE_TOK, BF16)
    return xs[:, :n_lat]
```

```python
import functools
import math

import jax
import jax.numpy as jnp
from jax import lax
from jax.experimental import pallas as pl
from jax.experimental.pallas import tpu as pltpu

F32 = jnp.float32
BF16 = jnp.bfloat16

GRID_W = 64
ROPE_THETA = 10000.0
EPS = 1e-6
GN_EPS = 1e-5
A_DIM = 64
B_QK = 128
B_V = 256
C_DIM = 128
C_GROUP = 4
TOP_K = 2
K_SHIFT_TOK, K_SCALE_TOK, K_GATE_TOK, K_SHIFT_CH, K_SCALE_CH, K_GATE_CH = range(6)
LANES = 128
SUBLANES = 8
LOG2E = 1.4426950408889634
BOUND_SLACK = 1.02
DEN_FLOOR = 2.0 ** -88
VMEM_LIMIT_BYTES = 56 * 1024 * 1024


def _pick(n, target, mult):
    best = None
    for t in range(mult, min(n, target) + 1, mult):
        if n % t == 0:
            best = t
    return best if best is not None else n


def _params(n_axes):
    return pltpu.CompilerParams(dimension_semantics=("arbitrary",) * n_axes,
                                vmem_limit_bytes=VMEM_LIMIT_BYTES)


def _silu(v):
    return v * jax.nn.sigmoid(v)


def _rms(v, eps):
    return v * lax.rsqrt(jnp.mean(v * v, axis=-1, keepdims=True) + eps)


def _ctx_rows(tile_idx, tm, n_lat, has_ctx):
    if not has_ctx:
        return None
    row = tile_idx * tm + lax.broadcasted_iota(jnp.int32, (tm, 1), 0)
    return row >= n_lat


def _mod_vec(ml_ref, mc_ref, k, is_ctx):
    return ml_ref[k] if is_ctx is None else jnp.where(is_ctx, mc_ref[k], ml_ref[k])


def _modulated_norm(x, g_ref, ml_ref, mc_ref, k_shift, k_scale, is_ctx):
    y = _rms(x, EPS) * g_ref[...]
    return y * (1.0 + _mod_vec(ml_ref, mc_ref, k_scale, is_ctx)) + _mod_vec(ml_ref, mc_ref, k_shift, is_ctx)


def _gated_residual(x, f, g_ref, ml_ref, mc_ref, k_gate, is_ctx):
    return x + _mod_vec(ml_ref, mc_ref, k_gate, is_ctx) * (_rms(f, EPS) * g_ref[...])


def _modvec_kernel(c_ref, w_ref, b_ref, o_ref):
    a = _silu(c_ref[...]).astype(BF16)
    o_ref[...] = jnp.dot(a, w_ref[...].astype(BF16), preferred_element_type=F32) + b_ref[...]


def _modvec(cond, w_mod, b_mod):
    depth, d, n6 = w_mod.shape
    rows = cond.shape[0]
    tn = _pick(n6, 1024, LANES)
    return pl.pallas_call(
        _modvec_kernel,
        grid=(depth, n6 // tn),
        in_specs=[pl.BlockSpec((rows, d), lambda i, n: (0, 0)),
                  pl.BlockSpec((None, d, tn), lambda i, n: (i, 0, n)),
                  pl.BlockSpec((None, 1, tn), lambda i, n: (i, 0, n))],
        out_specs=pl.BlockSpec((None, rows, tn), lambda i, n: (i, 0, n)),
        out_shape=jax.ShapeDtypeStruct((depth, rows, n6), F32),
        compiler_params=_params(2),
    )(cond, w_mod, b_mod.reshape(depth, 1, n6))


def _norm_mod_kernel(x_ref, g_ref, ml_ref, mc_ref, o_ref, *, k_shift, k_scale, n_lat, tm, has_ctx):
    is_ctx = _ctx_rows(pl.program_id(1), tm, n_lat, has_ctx)
    o_ref[...] = _modulated_norm(x_ref[...], g_ref, ml_ref, mc_ref, k_shift, k_scale, is_ctx).astype(o_ref.dtype)


def _norm_mod(x, g, mods, rows, n_lat, k_shift, k_scale, out_dtype):
    n_batch, _, d = x.shape
    tm = _pick(rows, 512, 16)
    kern = functools.partial(_norm_mod_kernel, k_shift=k_shift, k_scale=k_scale, n_lat=n_lat, tm=tm,
                             has_ctx=rows > n_lat)
    return pl.pallas_call(
        kern,
        grid=(n_batch, rows // tm),
        in_specs=[pl.BlockSpec((None, tm, d), lambda b, j: (b, j, 0)),
                  pl.BlockSpec((1, d), lambda b, j: (0, 0)),
                  pl.BlockSpec((None, 6, 1, d), lambda b, j: (b, 0, 0, 0)),
                  pl.BlockSpec((None, 6, 1, d), lambda b, j: (n_batch, 0, 0, 0))],
        out_specs=pl.BlockSpec((None, tm, d), lambda b, j: (b, j, 0)),
        out_shape=jax.ShapeDtypeStruct((n_batch, rows, d), out_dtype),
        compiler_params=_params(2),
    )(x, g.reshape(1, d), mods, mods)


def _join_norm_kernel(x_ref, c_ref, g_ref, ml_ref, mc_ref, xs_ref, h_ref, *, n_lat, tm):
    is_ctx = _ctx_rows(pl.program_id(1), tm, n_lat, True)
    rows = jnp.where(is_ctx, c_ref[...], x_ref[...])
    xs_ref[...] = rows
    h_ref[...] = _modulated_norm(rows, g_ref, ml_ref, mc_ref, K_SHIFT_TOK, K_SCALE_TOK, is_ctx).astype(h_ref.dtype)


def _join_norm(x, ctx, g, mods):
    n_batch, n_lat, d = x.shape
    n_ctx = ctx.shape[1]
    tm = _pick(math.gcd(n_lat, n_ctx), 512, 16)
    lat_tiles, ctx_tiles = n_lat // tm, n_ctx // tm
    out_spec = pl.BlockSpec((None, tm, d), lambda b, j: (b, j, 0))
    return pl.pallas_call(
        functools.partial(_join_norm_kernel, n_lat=n_lat, tm=tm),
        grid=(n_batch, lat_tiles + ctx_tiles),
        in_specs=[pl.BlockSpec((None, tm, d), lambda b, j: (b, jnp.minimum(j, lat_tiles - 1), 0)),
                  pl.BlockSpec((None, tm, d), lambda b, j: (b, jnp.maximum(j - lat_tiles, 0), 0)),
                  pl.BlockSpec((1, d), lambda b, j: (0, 0)),
                  pl.BlockSpec((None, 6, 1, d), lambda b, j: (b, 0, 0, 0)),
                  pl.BlockSpec((None, 6, 1, d), lambda b, j: (n_batch, 0, 0, 0))],
        out_specs=[out_spec, out_spec],
        out_shape=[jax.ShapeDtypeStruct((n_batch, n_lat + n_ctx, d), F32),
                   jax.ShapeDtypeStruct((n_batch, n_lat + n_ctx, d), BF16)],
        compiler_params=_params(2),
    )(x, ctx, g.reshape(1, d), mods, mods)


def _rope_tables(n_lat, n_ctx):
    rows = n_lat // GRID_W
    row = jnp.repeat(jnp.arange(rows, dtype=F32), GRID_W)
    col = jnp.tile(jnp.arange(GRID_W, dtype=F32), rows)

    def angles(dim):
        n = dim // 4
        inv = ROPE_THETA ** (-jnp.arange(n, dtype=F32) / n)
        return jnp.concatenate([row[:, None] * inv, col[:, None] * inv], axis=-1)

    a64, a128 = angles(A_DIM), angles(C_DIM)
    lane = jnp.arange(LANES)
    c64, s64 = jnp.cos(a64)[:, lane % 32], jnp.sin(a64)[:, lane % 32]
    first_half = (lane % 64) < 32
    sa64 = jnp.where(first_half, -s64, 0.0)
    sb64 = jnp.where(first_half, 0.0, s64)
    c128 = jnp.cos(a128)[:, lane % 64]
    s128 = jnp.sin(a128)[:, lane % 64] * jnp.where(lane < 64, -1.0, 1.0)
    lat = jnp.stack([c64, sa64, sb64, c128, s128])
    ident = jnp.zeros((5, n_ctx, LANES), F32).at[jnp.array([0, 3])].set(1.0)
    return jnp.concatenate([lat, ident], axis=1)


def _proj_epilogue(kind, scale, x, tab_ref, qkg_ref):
    name = kind[0]
    if name == "none":
        return x
    if name == "silu":
        return _silu(x)
    if name == "rope64":
        y = (x * tab_ref[0] + pltpu.roll(x, 96, 1) * tab_ref[1] + pltpu.roll(x, 32, 1) * tab_ref[2])
        return y * scale
    if name == "norm_rope128":
        x = _rms(x, EPS) * qkg_ref[kind[1]:kind[1] + 1, :]
    y = x * tab_ref[3] + pltpu.roll(x, 64, 1) * tab_ref[4]
    return y * scale


def _proj_kernel(h_ref, w_ref, tab_ref, qkg_ref, o_ref, acc_ref, *, groups, n_tiles):
    n = pl.program_id(2)
    slot = lax.rem(n, 2)

    def matmul():
        acc_ref[slot] = jnp.dot(h_ref[...], w_ref[...], preferred_element_type=F32)

    def epilogue(tiles, kinds, scales):
        prev = acc_ref.at[1 - slot]
        for c, kind in enumerate(kinds):
            scale = scales[0][c]
            for t, tile_scales in zip(tiles[1:], scales[1:]):
                if tile_scales[c] != scales[0][c]:
                    scale = jnp.where(n - 1 == t, tile_scales[c], scale)
            sl = slice(c * LANES, (c + 1) * LANES)
            o_ref[:, sl] = _proj_epilogue(kind, scale, prev[:, sl], tab_ref, qkg_ref).astype(o_ref.dtype)

    pl.when(n == 0)(matmul)
    for group in groups:
        tiles = group[0]
        prev_in_group = functools.reduce(jnp.logical_or, [n - 1 == t for t in tiles])

        @pl.when(prev_in_group & (n < n_tiles))
        def _(group=group):
            epilogue(*group)
            matmul()

        if n_tiles - 1 in tiles:
            pl.when(n == n_tiles)(functools.partial(epilogue, *group))


def _proj_layout(d):
    a_heads, b_heads, c_heads = d // 256, d // 512, d // 256
    c_kv = c_heads // C_GROUP
    segs = [("aq", a_heads, ("rope64", A_DIM ** -0.5 * LOG2E)), ("ak", a_heads, ("rope64", 1.0)),
            ("av", a_heads, ("none",)),
            ("bq", b_heads, ("rope128", 1.0)), ("bk", b_heads, ("rope128", B_QK ** -0.5)),
            ("bv", 2 * b_heads, ("none",)), ("bg", 2 * b_heads, ("silu",)),
            ("cq", c_heads, ("norm_rope128", C_DIM ** -0.5 * LOG2E, 0)), ("ck", c_kv, ("norm_rope128", 1.0, 1)),
            ("cv", c_kv, ("none",))]
    kinds, off = [], {}
    for name, nblk, kind in segs:
        off[name] = len(kinds)
        kinds += [kind] * nblk
    return kinds, off


def _proj(h, w, tabs, qkg, kinds):
    n_batch, rows, d = h.shape
    width = w.shape[1]
    tm = _pick(rows, 1152, 16)
    tn = _pick(width, 512, LANES)
    per = tn // LANES
    by_kinds = {}
    for t in range(width // tn):
        blocks = kinds[t * per:(t + 1) * per]
        shape = tuple((k[0],) + tuple(k[2:]) for k in blocks)
        by_kinds.setdefault(shape, []).append((t, tuple(k[1] if len(k) > 1 else 1.0 for k in blocks)))
    groups = tuple((tuple(t for t, _ in members), shape, tuple(sc for _, sc in members))
                   for shape, members in by_kinds.items())
    n_tiles = width // tn
    return pl.pallas_call(
        functools.partial(_proj_kernel, groups=groups, n_tiles=n_tiles),
        grid=(n_batch, rows // tm, n_tiles + 1),
        in_specs=[pl.BlockSpec((None, tm, d), lambda b, j, n: (b, j, 0)),
                  pl.BlockSpec((d, tn), lambda b, j, n: (0, jnp.minimum(n, n_tiles - 1))),
                  pl.BlockSpec((5, tm, LANES), lambda b, j, n: (0, j, 0)),
                  pl.BlockSpec((2, LANES), lambda b, j, n: (0, 0))],
        out_specs=pl.BlockSpec((None, tm, tn), lambda b, j, n: (b, j, jnp.maximum(n - 1, 0))),
        out_shape=jax.ShapeDtypeStruct((n_batch, rows, width), BF16),
        scratch_shapes=[pltpu.VMEM((2, tm, tn), F32)],
        compiler_params=_params(3),
    )(h, w, tabs, qkg)


def _scores(q, k):
    return lax.dot_general(q, k, (((1,), (1,)), ((), ())), preferred_element_type=F32)


def _softmax_pv_rowmax(q, k, vx):
    s = _scores(q, k)
    e = jnp.exp2(s - jnp.max(s, axis=-1, keepdims=True)).astype(BF16)
    r = jnp.dot(e, vx, preferred_element_type=F32)
    return r[:, :LANES] / r[:, LANES:LANES + 1]


def _bound_shift(q, ksq_max):
    qf = q.astype(F32)
    return BOUND_SLACK * jnp.sqrt(jnp.sum(qf * qf, axis=-1, keepdims=True) * ksq_max)


def _attn_kernel(q_ref, k_ref, v_ref, lam_ref, g_ref, o_ref, vx_ref, ksq_ref, *,
                 diff, lam_init, n_lat_tiles, n_lat, has_ctx, heads, kv_shared):
    i = pl.program_id(2)
    n_kv = 1 if kv_shared else heads
    n_keys = vx_ref.shape[1]

    def lanes(j):
        return slice(j * LANES, (j + 1) * LANES)

    @pl.when(i == 0)
    def _():
        for j in range(n_kv):
            vx_ref[j, :, :LANES] = v_ref[:, lanes(j)]
            vx_ref[j, :, LANES:] = jnp.ones((n_keys, LANES), BF16)
            k = k_ref[:, lanes(j)]
            ksq = jnp.dot(k * k, jnp.ones((LANES, LANES), BF16), preferred_element_type=F32)
            ksq_ref[j] = jnp.broadcast_to(jnp.max(ksq, axis=0, keepdims=True), (SUBLANES, LANES))

    def run(key_rows, nq):
        def operands(hh):
            j = 0 if kv_shared else hh
            q = q_ref[:nq, lanes(hh)]
            if diff:
                qf = q.astype(F32)
                lane = lax.broadcasted_iota(jnp.int32, qf.shape, 1)
                qs = (jnp.where(lane < A_DIM, qf, 0.0).astype(BF16), jnp.where(lane >= A_DIM, qf, 0.0).astype(BF16))
            else:
                qs = (q,)
            return qs, k_ref[key_rows, lanes(j)], vx_ref[j, key_rows, :], ksq_ref[j, 0:1, 0:1]

        def finish(hh, outs):
            if diff:
                lv = lam_ref[...]
                lam = (jnp.exp(jnp.sum(lv[0:1] * lv[1:2], axis=-1, keepdims=True))
                       - jnp.exp(jnp.sum(lv[2:3] * lv[3:4], axis=-1, keepdims=True)) + lam_init)
                o = _rms(outs[0] - lam * outs[1], EPS) * g_ref[...] * (1.0 - lam_init)
            else:
                o = outs[0]
            o_ref[:nq, lanes(hh)] = o.astype(o_ref.dtype)

        ops = [(hh, qq, k, vx, ksq) for hh in range(heads) for qs, k, vx, ksq in [operands(hh)] for qq in qs]
        shifted = [_scores(qq, k) - _bound_shift(qq, ksq) for _, qq, k, _, ksq in ops]
        exps = [jnp.exp2(s).astype(BF16) for s in shifted]
        prods = [jnp.dot(e, op[3], preferred_element_type=F32) for e, op in zip(exps, ops)]
        parts = [[] for _ in range(heads)]
        for op, r in zip(ops, prods):
            parts[op[0]].append((r[:, :LANES], r[:, LANES:LANES + 1]))
        den_min = functools.reduce(jnp.minimum, [den for head in parts for _, den in head])
        precise = jnp.min(den_min) >= DEN_FLOOR

        @pl.when(precise)
        def _():
            for hh in range(heads):
                finish(hh, [num / den for num, den in parts[hh]])

        @pl.when(jnp.logical_not(precise))
        def _():
            for hh in range(heads):
                qs, k, vx, _ = operands(hh)
                finish(hh, [_softmax_pv_rowmax(qq, k, vx) for qq in qs])

    tq = q_ref.shape[0]
    if not has_ctx:
        run(slice(None), tq)
        return

    @pl.when(i < n_lat_tiles)
    def _():
        run(slice(None), tq)

    @pl.when(i >= n_lat_tiles)
    def _():
        run(slice(n_lat, None), min(tq, n_keys - n_lat))


def _query_tile(n_lat, n_ctx, has_ctx):
    tq = _pick(n_lat, 512, 16)
    assert not has_ctx or n_ctx % tq == 0 or n_ctx < tq
    return tq


def _attention(p, lam_vec, subln_g, rows, n_lat, n_ctx, n_heads, q_off, k_off, v_off, kv_group, diff, lam_init):
    n_batch, n_keys, _ = p.shape
    has_ctx = rows > n_lat
    tq = _query_tile(n_lat, n_ctx, has_ctx)
    kv_shared = kv_group > 1
    heads = kv_group if kv_shared else 2
    assert n_heads % heads == 0 and q_off % heads == 0 and (kv_shared or (k_off % heads == 0 and v_off % heads == 0))
    kv_w = LANES if kv_shared else heads * LANES
    kv_blk = (lambda off: (lambda b, h, i: (b, 0, off + h))) if kv_shared else \
             (lambda off: (lambda b, h, i: (b, 0, off // heads + h)))
    kern = functools.partial(_attn_kernel, diff=diff, lam_init=lam_init, n_lat_tiles=n_lat // tq,
                             n_lat=n_lat, has_ctx=has_ctx, heads=heads, kv_shared=kv_shared)
    n_kv = 1 if kv_shared else heads
    return pl.pallas_call(
        kern,
        grid=(n_batch, n_heads // heads, pl.cdiv(rows, tq)),
        in_specs=[pl.BlockSpec((None, tq, heads * LANES), lambda b, h, i: (b, i, q_off // heads + h)),
                  pl.BlockSpec((None, n_keys, kv_w), kv_blk(k_off)),
                  pl.BlockSpec((None, n_keys, kv_w), kv_blk(v_off)),
                  pl.BlockSpec(lam_vec.shape, lambda b, h, i: (0, 0)),
                  pl.BlockSpec((1, LANES), lambda b, h, i: (0, 0))],
        out_specs=pl.BlockSpec((None, tq, heads * LANES), lambda b, h, i: (b, i, h)),
        out_shape=jax.ShapeDtypeStruct((n_batch, rows, n_heads * LANES), BF16),
        scratch_shapes=[pltpu.VMEM((n_kv, n_keys, 2 * LANES), BF16), pltpu.VMEM((n_kv, SUBLANES, LANES), F32)],
        compiler_params=_params(3),
    )(p, p, p, lam_vec, subln_g.reshape(1, LANES))


def _ret_decay(lgf, lgb, t, pf, pb):
    df = (t - pf).astype(F32)
    db = (pb - t).astype(F32)
    fwd = jnp.where(df >= 0, jnp.exp(lgf * jnp.maximum(df, 0.0)), 0.0)
    bwd = jnp.where(db >= 0, jnp.exp(lgb * jnp.maximum(db, 0.0)), 0.0)
    return fwd + bwd


def _ret_kernel(ld_ref, q_ref, k_ref, v_ref, g_ref, o_ref, d_ref, *, tq, n_keys, n_lat, n_heads, has_ctx, heads):
    hp, i, b = pl.program_id(0), pl.program_id(1), pl.program_id(2)
    n_ctx = n_keys - n_lat
    n_lat_tiles = n_lat // tq
    t_loc = lax.broadcasted_iota(jnp.int32, (tq, 1), 0)

    def log_decays(j):
        return ld_ref[hp * heads + j], ld_ref[n_heads + hp * heads + j]

    def lat_decay():
        col = lax.broadcasted_iota(jnp.int32, (1, n_keys), 1)
        pf = jnp.where(col < n_lat, col, col - n_keys)
        for j in range(heads):
            d_ref[j] = _ret_decay(*log_decays(j), i * tq + t_loc, pf, col)

    def ctx_decay():
        col = lax.broadcasted_iota(jnp.int32, (1, n_ctx), 1)
        for j in range(heads):
            d_ref[j, :, :n_ctx] = _ret_decay(*log_decays(j), (i - n_lat_tiles) * tq + t_loc, col, col)

    def run(key_rows, n_cols, nq):
        def qk(j):
            sl = slice(j * B_QK, (j + 1) * B_QK)
            return lax.dot_general(q_ref[:nq, sl], k_ref[key_rows, sl], (((1,), (1,)), ((), ())),
                                   preferred_element_type=F32)

        scores = [qk(j) for j in range(heads)]
        weighted = [(s * d_ref[j, :nq, :n_cols]).astype(BF16) for j, s in enumerate(scores)]
        for j, w in enumerate(weighted):
            sl = slice(j * B_V, (j + 1) * B_V)
            o = jnp.dot(w, v_ref[key_rows, sl], preferred_element_type=F32)
            mu = jnp.mean(o, axis=-1, keepdims=True)
            var = jnp.mean(jnp.square(o - mu), axis=-1, keepdims=True)
            o_ref[:nq, sl] = ((o - mu) * lax.rsqrt(var + GN_EPS) * g_ref[:nq, sl].astype(F32)).astype(o_ref.dtype)

    if not has_ctx:
        pl.when(b == 0)(lat_decay)
        run(slice(None), n_keys, tq)
        return

    pl.when((b == 0) & (i < n_lat_tiles))(lat_decay)
    pl.when((b == 0) & (i >= n_lat_tiles))(ctx_decay)

    @pl.when(i < n_lat_tiles)
    def _():
        run(slice(None), n_keys, tq)

    @pl.when(i >= n_lat_tiles)
    def _():
        run(slice(n_lat, None), n_ctx, min(tq, n_ctx))


def _retention(p, log_decay, rows, n_lat, n_ctx, n_heads, off):
    n_batch, n_keys, _ = p.shape
    has_ctx = rows > n_lat
    tq = _query_tile(n_lat, n_ctx, has_ctx)
    heads = 2 if n_heads % 2 == 0 else 1
    v_off, g_off = off["bv"] // 2, off["bg"] // 2
    assert all(o % heads == 0 for o in (off["bq"], off["bk"], v_off, g_off))
    kern = functools.partial(_ret_kernel, tq=tq, n_keys=n_keys, n_lat=n_lat, n_heads=n_heads, has_ctx=has_ctx,
                             heads=heads)
    grid_spec = pltpu.PrefetchScalarGridSpec(
        num_scalar_prefetch=1,
        grid=(n_heads // heads, pl.cdiv(rows, tq), n_batch),
        in_specs=[pl.BlockSpec((None, tq, heads * B_QK), lambda h, i, b, ld: (b, i, off["bq"] // heads + h)),
                  pl.BlockSpec((None, n_keys, heads * B_QK), lambda h, i, b, ld: (b, 0, off["bk"] // heads + h)),
                  pl.BlockSpec((None, n_keys, heads * B_V), lambda h, i, b, ld: (b, 0, v_off // heads + h)),
                  pl.BlockSpec((None, tq, heads * B_V), lambda h, i, b, ld: (b, i, g_off // heads + h))],
        out_specs=pl.BlockSpec((None, tq, heads * B_V), lambda h, i, b, ld: (b, i, h)),
        scratch_shapes=[pltpu.VMEM((heads, tq, n_keys), F32)])
    return pl.pallas_call(
        kern, grid_spec=grid_spec,
        out_shape=jax.ShapeDtypeStruct((n_batch, rows, n_heads * B_V), BF16),
        compiler_params=_params(3),
    )(log_decay.reshape(-1).astype(F32), p, p, p, p)


def _merge_kernel(h_ref, oa_ref, ob_ref, oc_ref, ga_ref, gb_ref, gc_ref, wb_ref, y_ref):
    h = h_ref[...]
    y = None
    for idx, (o_ref, wg_ref) in enumerate(((oa_ref, ga_ref), (ob_ref, gb_ref), (oc_ref, gc_ref))):
        gate = jax.nn.sigmoid(jnp.dot(h, wg_ref[...], preferred_element_type=F32))
        term = gate * jnp.dot(o_ref[...], wb_ref[idx], preferred_element_type=F32)
        y = term if y is None else y + term
    y_ref[...] = y.astype(y_ref.dtype)


def _merge(h, outs, w_gate, w_branch, rows):
    n_batch, _, d = h.shape
    bw = w_branch.shape[1]
    tm = _pick(rows, 1152, 16)
    tn = _pick(d, 512, LANES)
    nn = d // tn
    o_spec = pl.BlockSpec((None, tm, bw), lambda b, j, n: (b, j, 0))
    g_specs = [pl.BlockSpec((d, tn), functools.partial(lambda b, j, n, k: (0, k * nn + n), k=k)) for k in range(3)]
    return pl.pallas_call(
        _merge_kernel,
        grid=(n_batch, rows // tm, nn),
        in_specs=[pl.BlockSpec((None, tm, d), lambda b, j, n: (b, j, 0)), o_spec, o_spec, o_spec,
                  *g_specs, pl.BlockSpec((3, bw, tn), lambda b, j, n: (0, 0, n))],
        out_specs=pl.BlockSpec((None, tm, tn), lambda b, j, n: (b, j, n)),
        out_shape=jax.ShapeDtypeStruct((n_batch, rows, d), BF16),
        compiler_params=_params(3),
    )(h, *outs, w_gate, w_gate, w_gate, w_branch)


def _out_resid_kernel(*refs, tm, n_lat, has_ctx, n_experts):
    if n_experts:
        y_ref, w_ref, x_ref, g_ref, gn_ref, ml_ref, mc_ref, wr_ref, o_ref, hn_ref, idx_ref, wt_ref = refs
    else:
        y_ref, w_ref, x_ref, g_ref, gn_ref, ml_ref, mc_ref, o_ref, hn_ref = refs
    is_ctx = _ctx_rows(pl.program_id(1), tm, n_lat, has_ctx)
    f = jnp.dot(y_ref[...], w_ref[...], preferred_element_type=F32)
    x_new = _gated_residual(x_ref[...], f, g_ref, ml_ref, mc_ref, K_GATE_TOK, is_ctx)
    o_ref[...] = x_new
    h_next = _modulated_norm(x_new, gn_ref, ml_ref, mc_ref, K_SHIFT_CH, K_SCALE_CH, is_ctx)
    hn_ref[...] = h_next.astype(hn_ref.dtype)
    if n_experts:
        _route_top2(h_next, wr_ref, idx_ref, wt_ref, n_experts)


def _out_resid(y, w_out, x, g, g_next, mods, rows, n_lat, next_dtype, w_router=None):
    n_batch, _, d = y.shape
    tm = _pick(rows, 512, 16)
    n_experts = 0 if w_router is None else w_router.shape[1]
    kern = functools.partial(_out_resid_kernel, tm=tm, n_lat=n_lat, has_ctx=rows > n_lat, n_experts=n_experts)
    row_spec = pl.BlockSpec((None, tm, d), lambda b, j: (b, j, 0))
    vec_spec = pl.BlockSpec((1, d), lambda b, j: (0, 0))
    in_specs = [row_spec, pl.BlockSpec((d, d), lambda b, j: (0, 0)), row_spec, vec_spec, vec_spec,
                pl.BlockSpec((None, 6, 1, d), lambda b, j: (b, 0, 0, 0)),
                pl.BlockSpec((None, 6, 1, d), lambda b, j: (n_batch, 0, 0, 0))]
    args = [y, w_out, x, g.reshape(1, d), g_next.reshape(1, d), mods, mods]
    out_specs = [row_spec, row_spec]
    out_shape = [jax.ShapeDtypeStruct((n_batch, rows, d), F32), jax.ShapeDtypeStruct((n_batch, rows, d), next_dtype)]
    if n_experts:
        route_spec = pl.BlockSpec((None, tm, LANES), lambda b, j: (b, j, 0))
        in_specs.append(pl.BlockSpec((d, LANES), lambda b, j: (0, 0)))
        args.append(_pad_router(w_router))
        out_specs += [route_spec, route_spec]
        out_shape += [jax.ShapeDtypeStruct((n_batch, rows, LANES), jnp.int32),
                      jax.ShapeDtypeStruct((n_batch, rows, LANES), F32)]
    return pl.pallas_call(
        kern,
        grid=(n_batch, rows // tm),
        in_specs=in_specs, out_specs=out_specs, out_shape=out_shape,
        compiler_params=_params(2),
    )(*args)


def _ffn_kernel(*refs, tm, n_lat, has_ctx, has_next):
    if has_next:
        (h_ref, wg_ref, wu_ref, wo_ref, x_ref, g_ref, ml_ref, mc_ref, gn_ref, mln_ref, mcn_ref,
         o_ref, hn_ref) = refs
    else:
        h_ref, wg_ref, wu_ref, wo_ref, x_ref, g_ref, ml_ref, mc_ref, o_ref = refs
    f = pl.program_id(2)

    @pl.when(f == 0)
    def _():
        o_ref[...] = jnp.zeros_like(o_ref)

    h = h_ref[...]
    gate = jnp.dot(h, wg_ref[...], preferred_element_type=F32)
    up = jnp.dot(h, wu_ref[...], preferred_element_type=F32)
    o_ref[...] += jnp.dot((_silu(gate) * up).astype(BF16), wo_ref[...], preferred_element_type=F32)

    @pl.when(f == pl.num_programs(2) - 1)
    def _():
        is_ctx = _ctx_rows(pl.program_id(1), tm, n_lat, has_ctx)
        x_new = _gated_residual(x_ref[...], o_ref[...], g_ref, ml_ref, mc_ref, K_GATE_CH, is_ctx)
        o_ref[...] = x_new
        if has_next:
            hn_ref[...] = _modulated_norm(x_new, gn_ref, mln_ref, mcn_ref, K_SHIFT_TOK, K_SCALE_TOK,
                                          is_ctx).astype(hn_ref.dtype)


def _ffn_dense(h, w_in, w_out, x, g, mods, rows, n_lat, g_next, mods_next):
    n_batch, _, d = h.shape
    ff = w_out.shape[0]
    tm = _pick(rows, 576, 16)
    tf = _pick(ff, 512, LANES)
    nf = ff // tf
    has_next = g_next is not None
    kern = functools.partial(_ffn_kernel, tm=tm, n_lat=n_lat, has_ctx=rows > n_lat, has_next=has_next)
    row_spec = pl.BlockSpec((None, tm, d), lambda b, j, f: (b, j, 0))
    vec_spec = pl.BlockSpec((1, d), lambda b, j, f: (0, 0))
    mod_specs = [pl.BlockSpec((None, 6, 1, d), lambda b, j, f: (b, 0, 0, 0)),
                 pl.BlockSpec((None, 6, 1, d), lambda b, j, f: (n_batch, 0, 0, 0))]
    in_specs = [row_spec,
                pl.BlockSpec((d, tf), lambda b, j, f: (0, f)),
                pl.BlockSpec((d, tf), lambda b, j, f: (0, nf + f)),
                pl.BlockSpec((tf, d), lambda b, j, f: (f, 0)),
                row_spec, vec_spec, *mod_specs]
    args = [h, w_in, w_in, w_out, x, g.reshape(1, d), mods, mods]
    out_specs, out_shape = [row_spec], [jax.ShapeDtypeStruct((n_batch, rows, d), F32)]
    if has_next:
        in_specs += [vec_spec, *mod_specs]
        args += [g_next.reshape(1, d), mods_next, mods_next]
        out_specs.append(row_spec)
        out_shape.append(jax.ShapeDtypeStruct((n_batch, rows, d), BF16))
    outs = pl.pallas_call(
        kern,
        grid=(n_batch, rows // tm, nf),
        in_specs=in_specs, out_specs=out_specs, out_shape=out_shape,
        compiler_params=_params(3),
    )(*args)
    return (outs[0], outs[1]) if has_next else (outs[0], None)


def _route_top2(h, w_ref, idx_ref, wt_ref, n_experts):
    logits = jnp.dot(h.astype(BF16), w_ref[...], preferred_element_type=F32)
    lane = lax.broadcasted_iota(jnp.int32, logits.shape, 1)
    neg = jnp.float32(-jnp.inf)
    lg = jnp.where(lane < n_experts, logits, neg)
    m1 = jnp.max(lg, axis=-1, keepdims=True)
    i1 = jnp.min(jnp.where(lg == m1, lane, LANES), axis=-1, keepdims=True)
    lg2 = jnp.where(lane == i1, neg, lg)
    m2 = jnp.max(lg2, axis=-1, keepdims=True)
    i2 = jnp.min(jnp.where(lg2 == m2, lane, LANES), axis=-1, keepdims=True)
    e2 = jnp.exp(m2 - m1)
    den = 1.0 + e2
    idx_ref[...] = jnp.where(lane == 0, i1, jnp.where(lane == 1, i2, 0))
    wt_ref[...] = jnp.where(lane == 0, 1.0 / den, jnp.where(lane == 1, e2 / den, 0.0))


def _pad_router(w_router):
    d, n_experts = w_router.shape
    return jnp.zeros((d, LANES), BF16).at[:, :n_experts].set(w_router.astype(BF16))


def _route_plan(top_idx, n_experts, tm, n_tiles):
    n_tok = top_idx.shape[0]
    e_flat = top_idx.reshape(-1)
    onehot = (e_flat[:, None] == jnp.arange(n_experts, dtype=jnp.int32)[None, :]).astype(jnp.int32)
    csum = jnp.cumsum(onehot, axis=0)
    rank = jnp.sum((csum - onehot) * onehot, axis=1)
    tiles_e = (csum[-1] + tm - 1) // tm
    tiles_end = jnp.cumsum(tiles_e)
    pos = (tiles_end - tiles_e)[e_flat] * tm + rank
    n_used = tiles_end[-1]
    tile_ids = jnp.arange(n_tiles, dtype=jnp.int32)
    tile_expert = jnp.sum((tile_ids[:, None] >= tiles_end[None, :]).astype(jnp.int32), axis=1)
    tile_expert = jnp.minimum(tile_expert, n_experts - 1)
    last_expert = tile_expert[jnp.maximum(n_used - 1, 0)]
    tile_expert = jnp.where(tile_ids < n_used, tile_expert, last_expert)
    row_token = jnp.zeros((n_tiles * tm,), jnp.int32).at[pos].set(jnp.arange(TOP_K * n_tok, dtype=jnp.int32) // TOP_K)
    return row_token, pos.reshape(n_tok, TOP_K).astype(jnp.int32), tile_expert, n_used.reshape(1).astype(jnp.int32)


def _gather_kernel(tok_ref, tokn_ref, src_ref, o_ref, buf_ref, sems, *, tg):
    step = pl.program_id(0)
    slot = lax.rem(step, 2)

    def start_rows(t_ref, sl):
        def body(r, carry):
            pltpu.make_async_copy(src_ref.at[pl.ds(t_ref[0, r], 1)], buf_ref.at[sl, pl.ds(r, 1)], sems.at[sl]).start()
            return carry
        lax.fori_loop(0, tg, body, 0, unroll=8)

    pl.when(step == 0)(lambda: start_rows(tok_ref, slot))
    pl.when(step + 1 < pl.num_programs(0))(lambda: start_rows(tokn_ref, 1 - slot))
    pltpu.make_async_copy(src_ref.at[pl.ds(0, tg)], buf_ref.at[slot], sems.at[slot]).wait()
    o_ref[...] = buf_ref[slot].astype(o_ref.dtype)


def _gather_rows(src, row_token, tg, out_dtype):
    n_rows = row_token.shape[0]
    d = src.shape[1]
    n_steps = n_rows // tg
    tok = row_token.reshape(n_steps, 1, tg)
    tok_blk = (None, 1, tg)
    return pl.pallas_call(
        functools.partial(_gather_kernel, tg=tg),
        grid=(n_steps,),
        in_specs=[pl.BlockSpec(tok_blk, lambda i: (i, 0, 0), memory_space=pltpu.SMEM),
                  pl.BlockSpec(tok_blk, lambda i: (jnp.minimum(i + 1, n_steps - 1), 0, 0), memory_space=pltpu.SMEM),
                  pl.BlockSpec(memory_space=pl.ANY)],
        out_specs=pl.BlockSpec((tg, d), lambda i: (i, 0)),
        out_shape=jax.ShapeDtypeStruct((n_rows, d), out_dtype),
        scratch_shapes=[pltpu.VMEM((2, tg, d), src.dtype), pltpu.SemaphoreType.DMA((2,))],
        compiler_params=_params(1),
    )(tok, tok, src)


def _moe_ffn_kernel(te_ref, nv_ref, xs_ref, wg_ref, wu_ref, wo_ref, o_ref):
    i, f = pl.program_id(0), pl.program_id(1)
    valid = i < nv_ref[0]

    @pl.when(f == 0)
    def _():
        o_ref[...] = jnp.zeros_like(o_ref)

    @pl.when(valid)
    def _():
        xb = xs_ref[...]
        gate = jnp.dot(xb, wg_ref[...], preferred_element_type=F32)
        up = jnp.dot(xb, wu_ref[...], preferred_element_type=F32)
        o_ref[...] += jnp.dot((_silu(gate) * up).astype(BF16), wo_ref[...], preferred_element_type=F32)


def _moe_ffn(xs, w_in, w_out, tile_expert, n_used, tm):
    n_rows, d = xs.shape
    ff = w_out.shape[1]
    tf = _pick(ff, 512, LANES)
    nf = ff // tf

    def f_eff(i, f, nv):
        return jnp.where(i < nv[0], f, nf - 1)

    grid_spec = pltpu.PrefetchScalarGridSpec(
        num_scalar_prefetch=2,
        grid=(n_rows // tm, nf),
        in_specs=[pl.BlockSpec((tm, d), lambda i, f, te, nv: (jnp.minimum(i, jnp.maximum(nv[0] - 1, 0)), 0)),
                  pl.BlockSpec((None, d, tf), lambda i, f, te, nv: (te[i], 0, f_eff(i, f, nv))),
                  pl.BlockSpec((None, d, tf), lambda i, f, te, nv: (te[i], 0, nf + f_eff(i, f, nv))),
                  pl.BlockSpec((None, tf, d), lambda i, f, te, nv: (te[i], f_eff(i, f, nv), 0))],
        out_specs=pl.BlockSpec((tm, d), lambda i, f, te, nv: (i, 0)))
    return pl.pallas_call(
        _moe_ffn_kernel, grid_spec=grid_spec,
        out_shape=jax.ShapeDtypeStruct((n_rows, d), F32),
        compiler_params=_params(2),
    )(tile_expert, n_used, xs, w_in, w_in, w_out)


def _combine_kernel(pos_ref, posn_ref, ys_ref, wt_ref, x_ref, g_ref, ml_ref, mc_ref, o_ref, buf_ref, sems, *,
                    tc, n_lat, has_ctx):
    step = pl.program_id(0) * pl.num_programs(1) + pl.program_id(1)
    n_steps = pl.num_programs(0) * pl.num_programs(1)
    slot = lax.rem(step, 2)

    def start_rows(p_ref, sl):
        def body(r, carry):
            for k in range(TOP_K):
                pltpu.make_async_copy(ys_ref.at[pl.ds(p_ref[0, TOP_K * r + k], 1)],
                                      buf_ref.at[sl, k, pl.ds(r, 1)], sems.at[sl]).start()
            return carry
        lax.fori_loop(0, tc, body, 0, unroll=4)

    pl.when(step == 0)(lambda: start_rows(pos_ref, slot))
    pl.when(step + 1 < n_steps)(lambda: start_rows(posn_ref, 1 - slot))
    for k in range(TOP_K):
        pltpu.make_async_copy(ys_ref.at[pl.ds(0, tc)], buf_ref.at[slot, k], sems.at[slot]).wait()
    wt = wt_ref[...]
    f = wt[:, 0:1] * buf_ref[slot, 0]
    for k in range(1, TOP_K):
        f = f + wt[:, k:k + 1] * buf_ref[slot, k]
    is_ctx = _ctx_rows(pl.program_id(1), tc, n_lat, has_ctx)
    o_ref[...] = _gated_residual(x_ref[...], f, g_ref, ml_ref, mc_ref, K_GATE_CH, is_ctx)


def _combine(ys, pos, wts, x, g, mods, rows, n_lat):
    n_batch, _, d = x.shape
    tc = _pick(rows, 256, 8)
    nj = rows // tc
    n_steps = n_batch * nj
    kern = functools.partial(_combine_kernel, tc=tc, n_lat=n_lat, has_ctx=rows > n_lat)
    row_spec = pl.BlockSpec((None, tc, d), lambda b, j: (b, j, 0))
    pos_blk = (None, 1, TOP_K * tc)
    pos_steps = pos.reshape(n_steps, 1, TOP_K * tc)
    return pl.pallas_call(
        kern,
        grid=(n_batch, nj),
        in_specs=[pl.BlockSpec(pos_blk, lambda b, j: (b * nj + j, 0, 0), memory_space=pltpu.SMEM),
                  pl.BlockSpec(pos_blk, lambda b, j: (jnp.minimum(b * nj + j + 1, n_steps - 1), 0, 0),
                               memory_space=pltpu.SMEM),
                  pl.BlockSpec(memory_space=pl.ANY),
                  pl.BlockSpec((tc, LANES), lambda b, j: (b * nj + j, 0)),
                  row_spec,
                  pl.BlockSpec((1, d), lambda b, j: (0, 0)),
                  pl.BlockSpec((None, 6, 1, d), lambda b, j: (b, 0, 0, 0)),
                  pl.BlockSpec((None, 6, 1, d), lambda b, j: (n_batch, 0, 0, 0))],
        out_specs=row_spec,
        out_shape=jax.ShapeDtypeStruct((n_batch, rows, d), F32),
        scratch_shapes=[pltpu.VMEM((2, TOP_K, tc, d), F32), pltpu.SemaphoreType.DMA((2,))],
        compiler_params=_params(2),
    )(pos_steps, pos_steps, ys, wts, x, g.reshape(1, d), mods, mods)


def _ffn_moe(h, idx, wts, n_experts, w_in, w_out, x, g, mods, rows, n_lat):
    n_batch, _, d = h.shape
    n_tok = n_batch * rows
    tm = min(768, max(16, (TOP_K * n_tok // n_experts) // 16 * 16))
    n_tiles = (TOP_K * n_tok + n_experts * (tm - 1)) // tm
    h_flat = h.reshape(n_tok, d)
    idx, wts = idx.reshape(n_tok, LANES), wts.reshape(n_tok, LANES)
    row_token, pos, tile_expert, n_used = _route_plan(idx[:, :TOP_K], n_experts, tm, n_tiles)
    xs = _gather_rows(h_flat, row_token, tm, BF16)
    ys = _moe_ffn(xs, w_in, w_out, tile_expert, n_used, tm)
    return _combine(ys, pos, wts, x, g, mods, rows, n_lat)


def kernel(x, c, ctx, c_ctx, w_mod, b_mod, norm_g, w_in, qk_norm_g, diff_lambda, diff_subln_g, ret_log_decay,
           w_branch, w_branch_gate, w_out, ffn_w_in, ffn_w_out, moe_router, moe_w_in, moe_w_out):
    n_batch, n_lat, d = x.shape
    n_ctx = ctx.shape[1]
    depth = w_mod.shape[0]
    a_heads, b_heads, c_heads = d // 256, d // 512, d // 256
    kinds, off = _proj_layout(d)

    mod_rows = (n_batch + 1 + 7) // 8 * 8
    cond = jnp.zeros((mod_rows, d), F32).at[:n_batch].set(c).at[n_batch].set(c_ctx)
    mods_all = _modvec(cond, w_mod, b_mod).reshape(depth, mod_rows, 6, 1, d)
    tabs = _rope_tables(n_lat, n_ctx)

    n_all = n_lat + n_ctx
    xs, h = _join_norm(x, ctx, norm_g[0, 0], mods_all[0])
    for i in range(depth):
        last = i == depth - 1
        rows = n_lat if last else n_all
        lam_init = 0.8 - 0.6 * math.exp(-0.3 * i)
        mods = mods_all[i]
        dense = i % 2 == 0
        j = i // 2
        p = _proj(h, w_in[i].astype(BF16), tabs, qk_norm_g[i], kinds)
        o_a = _attention(p, diff_lambda[i], diff_subln_g[i], rows, n_lat, n_ctx, a_heads,
                         off["aq"], off["ak"], off["av"], 1, True, lam_init)
        o_b = _retention(p, ret_log_decay[i], rows, n_lat, n_ctx, b_heads, off)
        o_c = _attention(p, diff_lambda[i], diff_subln_g[i], rows, n_lat, n_ctx, c_heads,
                         off["cq"], off["ck"], off["cv"], C_GROUP, False, lam_init)
        y = _merge(h, (o_a, o_b, o_c), w_branch_gate[i].astype(BF16), w_branch[i].astype(BF16), rows)
        if dense:
            xs, h2 = _out_resid(y, w_out[i].astype(BF16), xs, norm_g[i, 1], norm_g[i, 2], mods, rows, n_lat, BF16)
        else:
            xs, h2, idx, wts = _out_resid(y, w_out[i].astype(BF16), xs, norm_g[i, 1], norm_g[i, 2], mods, rows,
                                          n_lat, F32, moe_router[j])
        if dense:
            g_next, mods_next = (None, None) if last else (norm_g[i + 1, 0], mods_all[i + 1])
            xs, h = _ffn_dense(h2, ffn_w_in[j].astype(BF16), ffn_w_out[j].astype(BF16), xs, norm_g[i, 3], mods,
                               rows, n_lat, g_next, mods_next)
        else:
            xs = _ffn_moe(h2, idx, wts, moe_router[j].shape[1], moe_w_in[j].astype(BF16),
                          moe_w_out[j].astype(BF16), xs, norm_g[i, 3], mods, rows, n_lat)
            if not last:
                h = _norm_mod(xs, norm_g[i + 1, 0], mods_all[i + 1], n_all, n_lat, K_SHIFT_TOK, K_SCALE_TOK, BF16)
    return xs[:, :n_lat]
```
